```python
import math
import jax
import jax.numpy as jnp
from jax import lax
import numpy as np

D_MODEL = 1024
BATCH = 2
SEQ = 8192
DEPTH = 4
DEC_BATCH = 128
DEC_SEQ = 1
PAST_LEN = 2048
PAGE_SIZE = 128

BRANCH_WIDTH = D_MODEL // 2
M_HEADS = 4
M_WIDTH = BRANCH_WIDTH
M_DK = M_WIDTH // M_HEADS
M_DV = M_WIDTH // M_HEADS
M_CONV = 4
M_F_BIAS = 3.0
F_HEADS = 8
F_HD = BRANCH_WIDTH // F_HEADS
F_WIDTH = F_HEADS * F_HD
F_QBLOCK = 128
F_F_BIAS = 7.0
H_HEADS = 4
H_DK = BRANCH_WIDTH // H_HEADS
H_DV = BRANCH_WIDTH // H_HEADS
H_KW = H_HEADS * H_DK
H_VW = H_HEADS * H_DV
CHUNK = 64
D_FF = ((8 * D_MODEL // 3 + 127) // 128) * 128
FFN_CONV = 3
EPS = 1e-6

_SIZES = (M_WIDTH, M_WIDTH, M_WIDTH, M_HEADS, M_HEADS,
          F_WIDTH, F_WIDTH, F_WIDTH, F_HEADS,
          H_KW, H_KW, H_VW, H_VW,
          D_MODEL, D_MODEL, D_MODEL)
N_IN = int(sum(_SIZES))
SPLIT_IDX = tuple(int(v) for v in np.cumsum(_SIZES)[:-1])

kernel_name = 'hybrid_mlstm_fox_hgrn2_convffn_step'


def _rms(x, g):
    x32 = x.astype(jnp.float32)
    y = x32 * lax.rsqrt(jnp.mean(x32 * x32, axis=-1, keepdims=True) + EPS)
    return y * g.astype(jnp.float32)


def _head_rms(o, g):
    o = o.astype(jnp.float32)
    y = o * lax.rsqrt(jnp.mean(o * o, axis=-1, keepdims=True) + EPS)
    return y.reshape(o.shape[0], o.shape[1], -1) * g


def _causal_dwconv(x, buf, w, b):
    W = w.shape[0]
    L = x.shape[1]
    xc = jnp.concatenate([buf.astype(x.dtype), x], axis=1)
    out = b + sum(w[j] * xc[:, j:j + L] for j in range(W))
    return out, xc[:, L:]


def _to_chunks(a, cs):
    B, L = a.shape[0], a.shape[1]
    a = a.reshape(B, L // cs, cs, *a.shape[2:])
    return jnp.swapaxes(jnp.moveaxis(a, 1, 0), 2, 3)


def _from_chunks(a):
    a = jnp.moveaxis(jnp.swapaxes(a, 2, 3), 0, 1)
    return a.reshape(a.shape[0], -1, *a.shape[3:])


def _mlstm_chunked(q, k, v, ig, lf, C0, n0, m0):
    f32 = jnp.float32
    L = q.shape[1]
    cs = math.gcd(L, CHUNK)
    scale = q.shape[-1] ** -0.5
    xs = (_to_chunks(q.astype(f32) * scale, cs), _to_chunks(k.astype(f32), cs),
          _to_chunks(v.astype(f32), cs), _to_chunks(ig.astype(f32), cs), _to_chunks(lf.astype(f32), cs))
    tri = jnp.tril(jnp.ones((cs, cs), dtype=bool))

    def step(carry, inp):
        C, n, m = carry
        qc, kc, vc, ic, fc = inp
        F = jnp.cumsum(fc, axis=-1)
        D = jnp.where(tri, F[..., :, None] - F[..., None, :] + ic[..., None, :], -jnp.inf)
        inter = F + m[..., None]
        mt = jnp.maximum(D.max(-1), inter)
        A = jnp.einsum('bhtd,bhsd->bhts', qc, kc) * jnp.exp(D - mt[..., None])
        ei = jnp.exp(inter - mt)
        num = jnp.einsum('bhts,bhse->bhte', A, vc) + ei[..., None] * jnp.einsum('bhtd,bhde->bhte', qc, C)
        den = A.sum(-1) + ei * jnp.einsum('bhtd,bhd->bht', qc, n)
        h = num / jnp.maximum(jnp.abs(den), jnp.exp(-mt))[..., None]
        FL = F[..., -1]
        wl = FL[..., None] - F + ic
        mn = jnp.maximum(FL + m, wl.max(-1))
        wk = jnp.exp(wl - mn[..., None])
        dec = jnp.exp(FL + m - mn)
        C = dec[..., None, None] * C + jnp.einsum('bhs,bhsd,bhse->bhde', wk, kc, vc)
        n = dec[..., None] * n + jnp.einsum('bhs,bhsd->bhd', wk, kc)
        return (C, n, mn), h

    (C, n, m), hs = lax.scan(step, (C0.astype(f32), n0.astype(f32), m0.astype(f32)), xs)
    return _from_chunks(hs), C, n, m


def _gla_chunked(q, k, v, la, S0):
    f32 = jnp.float32
    L = q.shape[1]
    cs = math.gcd(L, CHUNK)
    xs = tuple(_to_chunks(a.astype(f32), cs) for a in (q, k, v, la))
    tri = jnp.tril(jnp.ones((cs, cs), dtype=bool))[:, :, None]

    def step(S, inp):
        qc, kc, vc, ac = inp
        b = jnp.cumsum(ac, axis=2)
        dec = jnp.exp(jnp.where(tri, b[:, :, :, None, :] - b[:, :, None, :, :], -jnp.inf))
        A = jnp.einsum('bhtd,bhsd,bhtsd->bhts', qc, kc, dec)
        o = jnp.einsum('bhts,bhse->bhte', A, vc) + jnp.einsum('bhtd,bhde->bhte', qc * jnp.exp(b), S)
        bL = b[:, :, -1]
        S = jnp.exp(bL)[..., None] * S + jnp.einsum('bhsd,bhse->bhde', kc * jnp.exp(bL[:, :, None, :] - b), vc)
        return S, o

    S, os_ = lax.scan(step, S0.astype(f32), xs)
    return _from_chunks(os_), S


def _fox_prompt(q, k, v, lf):
    f32 = jnp.float32
    B, L, H, d = q.shape
    qb = math.gcd(L, F_QBLOCK)
    nb = L // qb
    Fc = jnp.cumsum(lf, axis=1)
    kf = k.astype(f32)
    vf = v.astype(f32)
    qblk = jnp.moveaxis((q.astype(f32) * d ** -0.5).reshape(B, nb, qb, H, d), 1, 0)
    Fblk = jnp.moveaxis(Fc.reshape(B, nb, qb, H), 1, 0)
    FcT = jnp.swapaxes(Fc, 1, 2)
    kpos = jnp.arange(L)

    def blk(args):
        qi, Fi, i = args
        s = jnp.einsum('bthd,bshd->bhts', qi, kf) + jnp.swapaxes(Fi, 1, 2)[..., None] - FcT[:, :, None, :]
        qpos = i * qb + jnp.arange(qb)
        s = jnp.where(kpos[None, :] <= qpos[:, None], s, -jnp.inf)
        p = jax.nn.softmax(s, axis=-1)
        return jnp.einsum('bhts,bshd->bthd', p, vf)

    out = lax.map(blk, (qblk, Fblk, jnp.arange(nb)))
    return jnp.moveaxis(out, 0, 1).reshape(B, L, H, d)


def _fox_sample(q, k, v, lf, kp, vp, lfp):
    f32 = jnp.float32
    T, d = q.shape[1], q.shape[-1]
    P = kp.shape[1]
    qs = q.astype(f32) * d ** -0.5
    lfp = lfp.astype(f32)
    Rp = lax.cumsum(lfp, axis=1, reverse=True) - lfp
    Gt = jnp.swapaxes(jnp.cumsum(lf, axis=1), 1, 2)
    sp = jnp.einsum('bthd,bshd->bhts', qs, kp.astype(f32)) + Gt[..., None] + jnp.swapaxes(Rp, 1, 2)[:, :, None, :]
    sn = jnp.einsum('bthd,bshd->bhts', qs, k.astype(f32)) + Gt[..., None] - Gt[:, :, None, :]
    sn = jnp.where(jnp.tril(jnp.ones((T, T), dtype=bool)), sn, -jnp.inf)
    p = jax.nn.softmax(jnp.concatenate([sp, sn], axis=-1), axis=-1)
    return (jnp.einsum('bhts,bshd->bthd', p[..., :P], vp.astype(f32))
            + jnp.einsum('bhts,bshd->bthd', p[..., P:], v.astype(f32)))


def _mixer(h, p, l, past):
    f32 = jnp.float32
    B, L, _ = h.shape
    z = h @ p['w_in'][l]
    (mx, mv, mo, mi, mf, fq, fk, fv, ff, hq, hf, hi, hg, ga, gb, gc) = jnp.split(z, SPLIT_IDX, axis=-1)

    mbuf = jnp.zeros((B, M_CONV - 1, M_WIDTH), z.dtype) if past is None else past['mconv']
    mcv, mconv_new = _causal_dwconv(mx, mbuf, p['m_conv_w'][l], p['m_conv_b'][l])
    ca = jax.nn.silu(mcv)
    cah = ca.reshape(B, L, M_HEADS, M_DK)
    mq = jnp.einsum('blhd,hde->blhe', cah, p['w_mq'][l])
    mk = jnp.einsum('blhd,hde->blhe', cah, p['w_mk'][l])
    mvh = mv.reshape(B, L, M_HEADS, M_DV)
    mgb = p['m_gate_b'][l]
    ig = (mi + mgb[:M_HEADS]).astype(f32)
    lfm = jax.nn.log_sigmoid((mf + mgb[M_HEADS:]).astype(f32))
    if past is None:
        C0 = jnp.zeros((B, M_HEADS, M_DK, M_DV), f32)
        n0 = jnp.zeros((B, M_HEADS, M_DK), f32)
        m0 = jnp.zeros((B, M_HEADS), f32)
    else:
        C0, n0, m0 = past['mc'], past['mn'], past['mm']
    hm, C1, n1, m1 = _mlstm_chunked(mq, mk, mvh, ig, lfm, C0, n0, m0)
    hm = (_head_rms(hm, p['m_hnorm'][l]) + p['m_skip'][l] * ca) * jax.nn.silu(mo)
    ya = hm @ p['w_pa'][l]

    q = fq.reshape(B, L, F_HEADS, F_HD)
    k = fk.reshape(B, L, F_HEADS, F_HD)
    v = fv.reshape(B, L, F_HEADS, F_HD)
    lff = jax.nn.log_sigmoid((ff + p['f_gate_b'][l]).astype(f32))
    if past is None:
        att = _fox_prompt(q, k, v, lff)
    else:
        att = _fox_sample(q, k, v, lff, past['k'], past['v'], past['logf'])
    yb = att.reshape(B, L, F_WIDTH) @ p['w_pb'][l]

    pl = jax.nn.softmax(p['h_lb_logits'].astype(f32), axis=0)
    cum = jnp.cumsum(pl, axis=0)
    lb = cum[l] - cum[0]
    hf32 = hf.astype(f32)
    la = jnp.log(lb + (1.0 - lb) * jax.nn.sigmoid(hf32)).reshape(B, L, H_HEADS, H_DK)
    hk = ((1.0 - lb) * jax.nn.sigmoid(-hf32)).reshape(B, L, H_HEADS, H_DK)
    hqh = jax.nn.silu(hq).reshape(B, L, H_HEADS, H_DK)
    hvh = hi.reshape(B, L, H_HEADS, H_DV)
    S0 = jnp.zeros((B, H_HEADS, H_DK, H_DV), f32) if past is None else past['hs']
    oh, S1 = _gla_chunked(hqh, hk, hvh, la, S0)
    oh = _head_rms(oh, p['h_hnorm'][l]) * jax.nn.silu(hg)
    yc = oh @ p['w_pc'][l]

    merged = jax.nn.sigmoid(ga) * ya + jax.nn.sigmoid(gb) * yb + jax.nn.sigmoid(gc) * yc
    out = merged @ p['w_out'][l]
    return out, (k, v, lff, C1, n1, m1, mconv_new, S1)


def _conv_ffn(h, p, l, buf):
    B = h.shape[0]
    g = h @ p['w_ffn_g'][l]
    u = h @ p['w_ffn_u'][l]
    if buf is None:
        buf = jnp.zeros((B, FFN_CONV - 1, D_FF), g.dtype)
    g, new_buf = _causal_dwconv(g, buf, p['ffn_conv_w'][l], p['ffn_conv_b'][l])
    return (jax.nn.gelu(g) * u) @ p['w_ffn_d'][l], new_buf


def _trunk_layer(x, c, p, l, past):
    ada = jax.nn.silu(c.astype(jnp.float32)) @ p['w_ada'][l] + p['b_ada'][l]
    sh1, sc1, g1, sh2, sc2, g2 = jnp.split(ada[:, None, :], 6, axis=-1)
    nw = p['norm_w'][l]
    h = _rms(x, nw[0]) * (1.0 + sc1) + sh1
    mix, mst = _mixer(h, p, l, past)
    x = (x + (1.0 + g1) * _rms(mix, nw[1])).astype(x.dtype)
    h2 = _rms(x, nw[2]) * (1.0 + sc2) + sh2
    ffo, fconv = _conv_ffn(h2, p, l, None if past is None else past['fconv'])
    x = (x + (1.0 + g2) * _rms(ffo, nw[3])).astype(x.dtype)
    return x, mst + (fconv,)


def setup_inputs(seed: int = 0) -> dict:
    key = jax.random.key(seed)
    ks = jax.random.split(key, 48)
    f32 = jnp.float32

    def nrm(k, shape, scale=1.0):
        return jax.random.normal(k, shape, f32) * scale

    n_pages = PAST_LEN // PAGE_SIZE
    n_used = DEC_BATCH * n_pages
    n_phys = n_used + max(1, n_used // 4)
    page_table = jax.random.permutation(ks[4], n_phys)[:n_used].reshape(DEC_BATCH, n_pages).astype(jnp.int32)
    m_gate_b = jnp.concatenate([nrm(ks[20], (DEPTH, M_HEADS), 0.1),
                                M_F_BIAS + nrm(ks[21], (DEPTH, M_HEADS), 0.5)], axis=-1)
    return {
        'x_prompt': nrm(ks[0], (BATCH, SEQ, D_MODEL)),
        'x_sample': nrm(ks[1], (DEC_BATCH, DEC_SEQ, D_MODEL)),
        'c_prompt': nrm(ks[2], (BATCH, D_MODEL)),
        'c_sample': nrm(ks[3], (DEC_BATCH, D_MODEL)),
        'cache_k': nrm(ks[5], (DEPTH, n_phys, PAGE_SIZE, F_HEADS, F_HD)),
        'cache_v': nrm(ks[6], (DEPTH, n_phys, PAGE_SIZE, F_HEADS, F_HD)),
        'cache_logf': jax.nn.log_sigmoid(nrm(ks[7], (DEPTH, n_phys, PAGE_SIZE, F_HEADS), 0.5) + F_F_BIAS),
        'page_table': page_table,
        'state_mlstm_c': nrm(ks[8], (DEPTH, DEC_BATCH, M_HEADS, M_DK, M_DV), 0.5),
        'state_mlstm_n': nrm(ks[9], (DEPTH, DEC_BATCH, M_HEADS, M_DK), 0.5),
        'state_mlstm_m': nrm(ks[10], (DEPTH, DEC_BATCH, M_HEADS)),
        'state_mlstm_conv': nrm(ks[11], (DEPTH, DEC_BATCH, M_CONV - 1, M_WIDTH)),
        'state_hgrn': nrm(ks[12], (DEPTH, DEC_BATCH, H_HEADS, H_DK, H_DV), 0.5),
        'state_ffn_conv': nrm(ks[13], (DEPTH, DEC_BATCH, FFN_CONV - 1, D_FF)),
        'norm_w': 1.0 + nrm(ks[14], (DEPTH, 4, D_MODEL), 0.05),
        'w_ada': nrm(ks[15], (DEPTH, D_MODEL, 6 * D_MODEL), 0.3 * D_MODEL ** -0.5),
        'b_ada': nrm(ks[16], (DEPTH, 6 * D_MODEL), 0.01),
        'w_in': nrm(ks[17], (DEPTH, D_MODEL, N_IN), D_MODEL ** -0.5),
        'm_conv_w': nrm(ks[18], (DEPTH, M_CONV, M_WIDTH), M_CONV ** -0.5),
        'm_conv_b': nrm(ks[19], (DEPTH, M_WIDTH), 0.01),
        'w_mq': nrm(ks[22], (DEPTH, M_HEADS, M_DK, M_DK), M_DK ** -0.5),
        'w_mk': nrm(ks[23], (DEPTH, M_HEADS, M_DK, M_DK), M_DK ** -0.5),
        'm_gate_b': m_gate_b,
        'm_skip': 1.0 + nrm(ks[24], (DEPTH, M_WIDTH), 0.05),
        'm_hnorm': 1.0 + nrm(ks[25], (DEPTH, M_WIDTH), 0.05),
        'f_gate_b': F_F_BIAS + nrm(ks[26], (DEPTH, F_HEADS), 0.5),
        'h_lb_logits': nrm(ks[27], (DEPTH, H_KW)),
        'h_hnorm': 1.0 + nrm(ks[28], (DEPTH, H_VW), 0.05),
        'w_pa': nrm(ks[29], (DEPTH, M_WIDTH, D_MODEL), M_WIDTH ** -0.5),
        'w_pb': nrm(ks[30], (DEPTH, F_WIDTH, D_MODEL), F_WIDTH ** -0.5),
        'w_pc': nrm(ks[31], (DEPTH, H_VW, D_MODEL), H_VW ** -0.5),
        'w_out': nrm(ks[32], (DEPTH, D_MODEL, D_MODEL), D_MODEL ** -0.5),
        'w_ffn_g': nrm(ks[33], (DEPTH, D_MODEL, D_FF), D_MODEL ** -0.5),
        'w_ffn_u': nrm(ks[34], (DEPTH, D_MODEL, D_FF), D_MODEL ** -0.5),
        'ffn_conv_w': nrm(ks[35], (DEPTH, FFN_CONV, D_FF), FFN_CONV ** -0.5),
        'ffn_conv_b': nrm(ks[36], (DEPTH, D_FF), 0.01),
        'w_ffn_d': nrm(ks[37], (DEPTH, D_FF, D_MODEL), D_FF ** -0.5),
    }


def reference(x_prompt, x_sample, c_prompt, c_sample, cache_k, cache_v, cache_logf, page_table,
              state_mlstm_c, state_mlstm_n, state_mlstm_m, state_mlstm_conv, state_hgrn, state_ffn_conv,
              norm_w, w_ada, b_ada, w_in, m_conv_w, m_conv_b, w_mq, w_mk, m_gate_b, m_skip, m_hnorm,
              f_gate_b, h_lb_logits, h_hnorm, w_pa, w_pb, w_pc, w_out,
              w_ffn_g, w_ffn_u, ffn_conv_w, ffn_conv_b, w_ffn_d):
    p = dict(norm_w=norm_w, w_ada=w_ada, b_ada=b_ada, w_in=w_in, m_conv_w=m_conv_w, m_conv_b=m_conv_b,
             w_mq=w_mq, w_mk=w_mk, m_gate_b=m_gate_b, m_skip=m_skip, m_hnorm=m_hnorm, f_gate_b=f_gate_b,
             h_lb_logits=h_lb_logits, h_hnorm=h_hnorm, w_pa=w_pa, w_pb=w_pb, w_pc=w_pc, w_out=w_out,
             w_ffn_g=w_ffn_g, w_ffn_u=w_ffn_u, ffn_conv_w=ffn_conv_w, ffn_conv_b=ffn_conv_b, w_ffn_d=w_ffn_d)
    n_st = 9

    x = x_prompt
    pst = [[] for _ in range(n_st)]
    for l in range(DEPTH):
        x, st = _trunk_layer(x, c_prompt, p, l, None)
        for j in range(n_st):
            pst[j].append(st[j])
    y_prompt = x
    P = [jnp.stack(s, axis=0) for s in pst]

    db, n_pages = page_table.shape
    ctx = n_pages * PAGE_SIZE
    x = x_sample
    sst = [[] for _ in range(n_st)]
    for l in range(DEPTH):
        past = dict(k=cache_k[l][page_table].reshape(db, ctx, F_HEADS, F_HD),
                    v=cache_v[l][page_table].reshape(db, ctx, F_HEADS, F_HD),
                    logf=cache_logf[l][page_table].reshape(db, ctx, F_HEADS),
                    mc=state_mlstm_c[l], mn=state_mlstm_n[l], mm=state_mlstm_m[l],
                    mconv=state_mlstm_conv[l], hs=state_hgrn[l], fconv=state_ffn_conv[l])
        x, st = _trunk_layer(x, c_sample, p, l, past)
        for j in range(n_st):
            sst[j].append(st[j])
    y_sample = x
    S = [jnp.stack(s, axis=0) for s in sst]

    return (y_prompt, y_sample,
            P[0], P[1], P[2], P[3], P[4], P[5], P[6], P[7], P[8],
            S[0], S[1], S[2], S[3], S[4], S[5], S[6], S[7], S[8])
```

```python
import functools
import math

import numpy as np
import jax
import jax.numpy as jnp
from jax import lax
from jax.experimental import pallas as pl
from jax.experimental.pallas import tpu as pltpu

F32 = jnp.float32
BF16 = jnp.bfloat16
HIGHEST = lax.Precision.HIGHEST

D_MODEL = 1024
DEPTH = 4
PAGE_SIZE = 128
BRANCH_WIDTH = D_MODEL // 2
M_HEADS = 4
M_DK = BRANCH_WIDTH // M_HEADS
M_CONV = 4
F_HEADS = 8
F_HD = BRANCH_WIDTH // F_HEADS
H_HEADS = 4
H_DK = BRANCH_WIDTH // H_HEADS
D_FF = ((8 * D_MODEL // 3 + 127) // 128) * 128
FFN_CONV = 3
EPS = 1e-6
NEG = -1e30

LANES = 128
VMEM_LIMIT = 56 * 1024 * 1024

_SIZES = (512, 512, 512, 4, 4, 512, 512, 512, 8, 512, 512, 512, 512, 1024, 1024, 1024)
_OFF = tuple(int(v) for v in np.concatenate([[0], np.cumsum(_SIZES)]))
BLK_MX, BLK_MV, BLK_MO, BLK_FQ, BLK_FK, BLK_FV, BLK_HQ, BLK_HF, BLK_HI, BLK_HG = range(10)
COL_GA, COL_GB, COL_GC = 5120, 6144, 7168
N_MAIN = 8192
LANE_MI, LANE_MF, LANE_FF = 0, 4, 8


def _nt(a, b):
    return lax.dot_general(a, b, (((1,), (1,)), ((), ())), preferred_element_type=F32)


def _dot(a, b):
    return jnp.dot(a, b, preferred_element_type=F32)


def _log_sigmoid(x):
    return jnp.minimum(x, 0.0) - jnp.log1p(jnp.exp(-jnp.abs(x)))


def _sigmoid(x):
    return 1.0 / (1.0 + jnp.exp(-x))


def _params(*sem):
    return pltpu.CompilerParams(dimension_semantics=sem, vmem_limit_bytes=VMEM_LIMIT)


def _mm_kernel(x_ref, w_ref, o_ref, xs):
    @pl.when(pl.program_id(1) == 0)
    def _():
        xs[...] = x_ref[...].astype(BF16)

    o_ref[...] = _dot(xs[...], w_ref[...])


def _mm(x, w, *, tm, tn):
    m, k = x.shape
    n = w.shape[1]
    return pl.pallas_call(
        _mm_kernel,
        grid=(m // tm, n // tn),
        in_specs=[pl.BlockSpec((tm, k), lambda i, j: (i, 0)),
                  pl.BlockSpec((k, tn), lambda i, j: (0, j))],
        out_specs=pl.BlockSpec((tm, tn), lambda i, j: (i, j)),
        out_shape=jax.ShapeDtypeStruct((m, n), F32),
        scratch_shapes=[pltpu.VMEM((tm, k), BF16)],
        compiler_params=_params("parallel", "arbitrary"),
        name="mm",
    )(x, w)


def _pmm_kernel(x_ref, mul_ref, add_ref, w_ref, o_ref, hs):
    @pl.when(pl.program_id(1) == 0)
    def _():
        x = x_ref[...]
        y = x * lax.rsqrt(jnp.mean(x * x, axis=-1, keepdims=True) + EPS)
        hs[...] = (y * mul_ref[0] + add_ref[0]).astype(BF16)

    o_ref[...] = _dot(hs[...], w_ref[...])


def _pmm(x, mul, add, w, *, tm, tn, tiles_per_group):
    m, k = x.shape
    n = w.shape[1]
    r = mul.shape[1]
    return pl.pallas_call(
        _pmm_kernel,
        grid=(m // tm, n // tn),
        in_specs=[pl.BlockSpec((tm, k), lambda i, j: (i, 0)),
                  pl.BlockSpec((1, r, k), lambda i, j: (i // tiles_per_group, 0, 0)),
                  pl.BlockSpec((1, r, k), lambda i, j: (i // tiles_per_group, 0, 0)),
                  pl.BlockSpec((k, tn), lambda i, j: (0, j))],
        out_specs=pl.BlockSpec((tm, tn), lambda i, j: (i, j)),
        out_shape=jax.ShapeDtypeStruct((m, n), F32),
        scratch_shapes=[pltpu.VMEM((tm, k), BF16)],
        compiler_params=_params("parallel", "arbitrary"),
        name="pmm",
    )(x, mul, add, w)


def _headmm_kernel(x_ref, w_ref, o_ref):
    o_ref[...] = _dot(x_ref[...].astype(BF16), w_ref[0])


def _headmm(x, w):
    m = x.shape[0]
    h, d, _ = w.shape
    return pl.pallas_call(
        _headmm_kernel,
        grid=(h,),
        in_specs=[pl.BlockSpec((m, d), lambda i: (0, i)),
                  pl.BlockSpec((1, d, d), lambda i: (i, 0, 0))],
        out_specs=pl.BlockSpec((m, d), lambda i: (0, i)),
        out_shape=jax.ShapeDtypeStruct((m, h * d), F32),
        compiler_params=_params("parallel"),
        name="headmm",
    )(x, w)


def _gates_kernel(zs_ref, bias_ref, mcum_ref, g_ref, cum_ref, carry):
    @pl.when(pl.program_id(1) == 0)
    def _():
        carry[...] = jnp.zeros_like(carry)

    z = zs_ref[0] + bias_ref[...]
    ls = _log_sigmoid(z)
    lane = lax.broadcasted_iota(jnp.int32, z.shape, 1)
    g_ref[0] = jnp.where(lane < LANE_MF, z, ls)
    cum = jnp.dot(mcum_ref[...], ls, precision=HIGHEST, preferred_element_type=F32) + carry[0:1, :]
    cum_ref[0] = cum
    tc = z.shape[0]
    carry[...] = jnp.broadcast_to(cum[tc - 1:tc, :], carry.shape)


def _gates(zs, bias, *, tc):
    b, l, _ = zs.shape
    mcum = jnp.asarray(np.tril(np.ones((tc, tc), np.float32)))
    return pl.pallas_call(
        _gates_kernel,
        grid=(b, l // tc),
        in_specs=[pl.BlockSpec((1, tc, LANES), lambda i, c: (i, c, 0)),
                  pl.BlockSpec((1, LANES), lambda i, c: (0, 0)),
                  pl.BlockSpec((tc, tc), lambda i, c: (0, 0))],
        out_specs=[pl.BlockSpec((1, tc, LANES), lambda i, c: (i, c, 0)),
                   pl.BlockSpec((1, tc, LANES), lambda i, c: (i, c, 0))],
        out_shape=[jax.ShapeDtypeStruct((b, l, LANES), F32),
                   jax.ShapeDtypeStruct((b, l, LANES), F32)],
        scratch_shapes=[pltpu.VMEM((8, LANES), F32)],
        compiler_params=_params("parallel", "arbitrary"),
        name="gates",
    )(zs, bias, mcum)


def _mlstm_kernel(mx_ref, mv_ref, mo_ref, gc_ref, gr_ref, cw_ref, cb_ref, wq_ref, wk_ref,
                  hn_ref, sk_ref, mcum_ref,
                  out_ref, cfin_ref, nfin_ref, mfin_ref,
                  xbuf, c_sc, n_sc, m_sc, *, c):
    ci = pl.program_id(1)

    @pl.when(ci == 0)
    def _():
        xbuf[0:8, :] = jnp.zeros((8, BRANCH_WIDTH), F32)
        c_sc[...] = jnp.zeros_like(c_sc)
        n_sc[...] = jnp.zeros_like(n_sc)
        m_sc[...] = jnp.zeros_like(m_sc)

    x = mx_ref[0]
    xbuf[8:8 + c, :] = x
    cw = cw_ref[...]
    mcv = (cb_ref[...] + cw[3:4, :] * x + cw[2:3, :] * xbuf[7:7 + c, :]
           + cw[1:2, :] * xbuf[6:6 + c, :] + cw[0:1, :] * xbuf[5:5 + c, :])
    xbuf[0:8, :] = x[c - 8:c, :]
    ca = mcv * _sigmoid(mcv)

    gcol = gc_ref[0]
    grow = gr_ref[0]
    mcum = mcum_ref[...]
    fcol = jnp.dot(mcum, gcol, precision=HIGHEST, preferred_element_type=F32)
    frow = lax.dot_general(grow, mcum, (((1,), (1,)), ((), ())), precision=HIGHEST,
                           preferred_element_type=F32)
    tri = (lax.broadcasted_iota(jnp.int32, (c, c), 0) >= lax.broadcasted_iota(jnp.int32, (c, c), 1))
    scale = M_DK ** -0.5
    mv = mv_ref[0]
    mo = mo_ref[0]

    for h in range(M_HEADS):
        sl = slice(h * M_DK, (h + 1) * M_DK)
        cah = ca[:, sl]
        cb16 = cah.astype(BF16)
        q = _dot(cb16, wq_ref[h]) * scale
        k = _dot(cb16, wk_ref[h])
        v = mv[:, sl]
        q16 = q.astype(BF16)
        v16 = v.astype(BF16)
        f_c = fcol[:, LANE_MF + h:LANE_MF + h + 1]
        ig_c = gcol[:, h:h + 1]
        f_r = frow[LANE_MF + h:LANE_MF + h + 1, :]
        ig_r = grow[h:h + 1, :]
        m_prev = m_sc[h:h + 1, 0:1]

        d = jnp.where(tri, f_c - f_r + ig_r, NEG)
        inter = f_c + m_prev
        mt = jnp.maximum(jnp.max(d, axis=-1, keepdims=True), inter)
        a = _nt(q16, k.astype(BF16)) * jnp.exp(d - mt)
        ei = jnp.exp(inter - mt)
        num = _dot(a.astype(BF16), v16) + ei * _dot(q16, c_sc[h].astype(BF16))
        den = jnp.sum(a, axis=-1, keepdims=True) + ei * jnp.sum(q * n_sc[h:h + 1, :], axis=-1, keepdims=True)
        hh = num / jnp.maximum(jnp.abs(den), jnp.exp(-mt))

        fl = f_c[c - 1:c, :]
        wl_c = fl - f_c + ig_c
        wl_r = fl - f_r + ig_r
        mn = jnp.maximum(fl + m_prev, jnp.max(wl_r, axis=-1, keepdims=True))
        dec = jnp.exp(fl + m_prev - mn)
        kw = k * jnp.exp(wl_c - mn)
        c_sc[h] = dec * c_sc[h] + _dot(kw.T.astype(BF16), v16)
        n_sc[h:h + 1, :] = dec * n_sc[h:h + 1, :] + jnp.sum(kw, axis=0, keepdims=True)
        m_sc[h:h + 1, :] = jnp.broadcast_to(mn, (1, LANES))

        hn = hh * lax.rsqrt(jnp.mean(hh * hh, axis=-1, keepdims=True) + EPS)
        hn = hn * hn_ref[:, sl] + sk_ref[:, sl] * cah
        moh = mo[:, sl]
        out_ref[0, :, sl] = hn * (moh * _sigmoid(moh))

    @pl.when(ci == pl.num_programs(1) - 1)
    def _():
        cfin_ref[0] = c_sc[...]
        nfin_ref[0] = n_sc[...]
        mfin_ref[0] = m_sc[...]


def _mlstm_prompt(z, gcol, grow, cw, cb, wq, wk, hnorm, skip, *, c):
    b, l, _ = z.shape
    mcum = jnp.asarray(np.tril(np.ones((c, c), np.float32)))
    w = BRANCH_WIDTH
    full = lambda shape: pl.BlockSpec(shape, lambda i, j: (0,) * len(shape))
    return pl.pallas_call(
        functools.partial(_mlstm_kernel, c=c),
        grid=(b, l // c),
        in_specs=[pl.BlockSpec((1, c, w), lambda i, j: (i, j, BLK_MX)),
                  pl.BlockSpec((1, c, w), lambda i, j: (i, j, BLK_MV)),
                  pl.BlockSpec((1, c, w), lambda i, j: (i, j, BLK_MO)),
                  pl.BlockSpec((1, c, LANES), lambda i, j: (i, j, 0)),
                  pl.BlockSpec((1, 8, c), lambda i, j: (i, 0, j)),
                  full((M_CONV, w)), full((1, w)),
                  full((M_HEADS, M_DK, M_DK)), full((M_HEADS, M_DK, M_DK)),
                  full((1, w)), full((1, w)), full((c, c))],
        out_specs=[pl.BlockSpec((1, c, w), lambda i, j: (i, j, 0)),
                   pl.BlockSpec((1, M_HEADS, M_DK, M_DK), lambda i, j: (i, 0, 0, 0)),
                   pl.BlockSpec((1, 8, LANES), lambda i, j: (i, 0, 0)),
                   pl.BlockSpec((1, 8, LANES), lambda i, j: (i, 0, 0))],
        out_shape=[jax.ShapeDtypeStruct((b, l, w), F32),
                   jax.ShapeDtypeStruct((b, M_HEADS, M_DK, M_DK), F32),
                   jax.ShapeDtypeStruct((b, 8, LANES), F32),
                   jax.ShapeDtypeStruct((b, 8, LANES), F32)],
        scratch_shapes=[pltpu.VMEM((c + 8, w), F32),
                        pltpu.VMEM((M_HEADS, M_DK, M_DK), F32),
                        pltpu.VMEM((8, LANES), F32),
                        pltpu.VMEM((8, LANES), F32)],
        compiler_params=_params("parallel", "arbitrary"),
        name="mlstm_prompt",
    )(z, z, z, gcol, grow, cw, cb, wq, wk, hnorm, skip, mcum)


def _fox_step(masked, q_ref, k_ref, v_ref, fq_ref, fk_ref, m_sc, l_sc, acc_sc, tq, tk):
    q = q_ref[0] * (F_HD ** -0.5)
    k = k_ref[0].astype(BF16)
    v = v_ref[0].astype(BF16)
    fq = fq_ref[0]
    fk = fk_ref[0]
    lo = lax.broadcasted_iota(jnp.int32, (tq, LANES), 1) < F_HD
    if masked:
        tri = (lax.broadcasted_iota(jnp.int32, (tq, tk), 0) >= lax.broadcasted_iota(jnp.int32, (tq, tk), 1))
    for pr in range(F_HEADS // 2):
        sl = slice(pr * LANES, (pr + 1) * LANES)
        q2, k2, v2 = q[:, sl], k[:, sl], v[:, sl]
        alphas, pvs = [], []
        for hh in range(2):
            h = 2 * pr + hh
            qm = jnp.where(lo if hh == 0 else jnp.logical_not(lo), q2, 0.0).astype(BF16)
            s = _nt(qm, k2) + fq[:, LANE_FF + h:LANE_FF + h + 1] - fk[h:h + 1, :]
            if masked:
                s = jnp.where(tri, s, NEG)
            m_prev = m_sc[:, h:h + 1]
            m_new = jnp.maximum(m_prev, jnp.max(s, axis=-1, keepdims=True))
            alpha = jnp.exp(m_prev - m_new)
            p = jnp.exp(s - m_new)
            l_sc[:, h:h + 1] = alpha * l_sc[:, h:h + 1] + jnp.sum(p, axis=-1, keepdims=True)
            m_sc[:, h:h + 1] = m_new
            pvs.append(_dot(p.astype(BF16), v2))
            alphas.append(alpha)
        acc_sc[pr] = acc_sc[pr] * jnp.where(lo, alphas[0], alphas[1]) + jnp.where(lo, pvs[0], pvs[1])


def _fox_kernel(qi_ref, kj_ref, q_ref, k_ref, v_ref, fq_ref, fk_ref, o_ref, m_sc, l_sc, acc_sc, *, tq, tk):
    s = pl.program_id(1)
    i = qi_ref[s]
    j = kj_ref[s]

    @pl.when(j == 0)
    def _():
        m_sc[...] = jnp.full(m_sc.shape, NEG, F32)
        l_sc[...] = jnp.zeros_like(l_sc)
        acc_sc[...] = jnp.zeros_like(acc_sc)

    @pl.when(j < i)
    def _():
        _fox_step(False, q_ref, k_ref, v_ref, fq_ref, fk_ref, m_sc, l_sc, acc_sc, tq, tk)

    @pl.when(j == i)
    def _():
        _fox_step(True, q_ref, k_ref, v_ref, fq_ref, fk_ref, m_sc, l_sc, acc_sc, tq, tk)
        lo = lax.broadcasted_iota(jnp.int32, (tq, LANES), 1) < F_HD
        for pr in range(F_HEADS // 2):
            inv = jnp.where(lo, 1.0 / l_sc[:, 2 * pr:2 * pr + 1], 1.0 / l_sc[:, 2 * pr + 1:2 * pr + 2])
            o_ref[0, :, pr * LANES:(pr + 1) * LANES] = acc_sc[pr] * inv


def _fox_prompt(z, cum, fkt, *, t):
    b, l, _ = z.shape
    nb = l // t
    qi = np.array([i for i in range(nb) for _ in range(i + 1)], np.int32)
    kj = np.array([j for i in range(nb) for j in range(i + 1)], np.int32)
    w = BRANCH_WIDTH
    grid_spec = pltpu.PrefetchScalarGridSpec(
        num_scalar_prefetch=2,
        grid=(b, len(qi)),
        in_specs=[pl.BlockSpec((1, t, w), lambda i, s, qi, kj: (i, qi[s], BLK_FQ)),
                  pl.BlockSpec((1, t, w), lambda i, s, qi, kj: (i, kj[s], BLK_FK)),
                  pl.BlockSpec((1, t, w), lambda i, s, qi, kj: (i, kj[s], BLK_FV)),
                  pl.BlockSpec((1, t, LANES), lambda i, s, qi, kj: (i, qi[s], 0)),
                  pl.BlockSpec((1, 8, t), lambda i, s, qi, kj: (i, 0, kj[s]))],
        out_specs=pl.BlockSpec((1, t, w), lambda i, s, qi, kj: (i, qi[s], 0)),
        scratch_shapes=[pltpu.VMEM((t, LANES), F32), pltpu.VMEM((t, LANES), F32),
                        pltpu.VMEM((F_HEADS // 2, t, LANES), F32)])
    return pl.pallas_call(
        functools.partial(_fox_kernel, tq=t, tk=t),
        grid_spec=grid_spec,
        out_shape=jax.ShapeDtypeStruct((b, l, w), F32),
        compiler_params=_params("parallel", "arbitrary"),
        name="fox_prompt",
    )(jnp.asarray(qi), jnp.asarray(kj), z, z, z, cum, fkt)


HG_C = 128
HG_LEVELS = 7


def _hgrn_mats(c):
    mats = [np.tril(np.ones((c, c), np.float32))]
    for lev in range(HG_LEVELS):
        s = (c // 2) >> lev
        m = np.zeros((c, c), np.float32)
        for t in range(c):
            ref = (t // (2 * s)) * 2 * s + s - 1
            if t % (2 * s) >= s:
                m[t, ref + 1:t + 1] = 1.0
            else:
                m[t, t + 1:ref + 1] = 1.0
        mats.append(m)
    mats.append(np.triu(np.ones((c, c), np.float32), 1))
    return np.concatenate(mats, axis=0)


def _hgrn_kernel(hq_ref, hf_ref, hi_ref, hg_ref, lb_ref, hn_ref, mall_ref, o_ref, sfin_ref, st_sc):
    c = HG_C
    ci = pl.program_id(2)

    @pl.when(ci == 0)
    def _():
        st_sc[...] = jnp.zeros_like(st_sc)

    hf = hf_ref[0]
    lb = lb_ref[...]
    la = jnp.log(lb + (1.0 - lb) * _sigmoid(hf))
    k = (1.0 - lb) * _sigmoid(-hf)
    hq = hq_ref[0]
    q = hq * _sigmoid(hq)
    v16 = hi_ref[0].astype(BF16)

    args = jnp.dot(mall_ref[...], la, precision=HIGHEST, preferred_element_type=F32)
    b = args[0:c]
    ti = lax.broadcasted_iota(jnp.int32, (c, c), 0)
    si = lax.broadcasted_iota(jnp.int32, (c, c), 1)
    rowi = lax.broadcasted_iota(jnp.int32, (c, H_DK), 0)
    a = jnp.where(ti == si, _nt(q.astype(BF16), k.astype(BF16)), 0.0)
    for lev in range(HG_LEVELS):
        sh = int(math.log2(c // 2)) - lev
        upper = ((rowi >> sh) & 1) == 1
        x16 = (jnp.where(upper, q, k) * jnp.exp(args[(lev + 1) * c:(lev + 2) * c])).astype(BF16)
        p = _nt(x16, x16)
        mask = ((ti >> (sh + 1)) == (si >> (sh + 1))) & (((ti >> sh) & 1) == 1) & (((si >> sh) & 1) == 0)
        a = a + jnp.where(mask, p, 0.0)

    st = st_sc[...]
    o = _dot(a.astype(BF16), v16) + _nt((q * jnp.exp(b)).astype(BF16), st.astype(BF16))
    kt = k * jnp.exp(args[(HG_LEVELS + 1) * c:(HG_LEVELS + 2) * c])
    st_new = st * jnp.exp(b[c - 1:c, :]) + _dot(hi_ref[0].T.astype(BF16), kt.astype(BF16))
    st_sc[...] = st_new

    hg = hg_ref[0]
    on = o * lax.rsqrt(jnp.mean(o * o, axis=-1, keepdims=True) + EPS)
    o_ref[0] = on * hn_ref[...] * (hg * _sigmoid(hg))

    @pl.when(ci == pl.num_programs(2) - 1)
    def _():
        sfin_ref[0, 0] = st_new.T


def _hgrn_prompt(z, lb, hnorm):
    b, l, _ = z.shape
    c = HG_C
    mall = jnp.asarray(_hgrn_mats(c))
    blk = lambda base: pl.BlockSpec((1, c, H_DK), lambda i, h, j: (i, j, base * 4 + h))
    return pl.pallas_call(
        _hgrn_kernel,
        grid=(b, H_HEADS, l // c),
        in_specs=[blk(BLK_HQ), blk(BLK_HF), blk(BLK_HI), blk(BLK_HG),
                  pl.BlockSpec((1, H_DK), lambda i, h, j: (0, h)),
                  pl.BlockSpec((1, H_DK), lambda i, h, j: (0, h)),
                  pl.BlockSpec(mall.shape, lambda i, h, j: (0, 0))],
        out_specs=[pl.BlockSpec((1, c, H_DK), lambda i, h, j: (i, j, h)),
                   pl.BlockSpec((1, 1, H_DK, H_DK), lambda i, h, j: (i, h, 0, 0))],
        out_shape=[jax.ShapeDtypeStruct((b, l, BRANCH_WIDTH), F32),
                   jax.ShapeDtypeStruct((b, H_HEADS, H_DK, H_DK), F32)],
        scratch_shapes=[pltpu.VMEM((H_DK, H_DK), F32)],
        compiler_params=_params("parallel", "parallel", "arbitrary"),
        name="hgrn_prompt",
    )(z, z, z, z, lb, hnorm, mall)


def _state_step_kernel(st_ref, at_ref, bt_ref, qt_ref, v_ref, so_ref, o_ref, *, gb):
    at = at_ref[0]
    bt = bt_ref[0]
    qt = qt_ref[0]
    v = v_ref[...]
    for j in range(gb):
        s_new = st_ref[j] * at[:, j:j + 1] + bt[:, j:j + 1] * v[j:j + 1, :]
        so_ref[j] = s_new
        o_ref[j:j + 1, :] = jnp.sum(s_new * qt[:, j:j + 1], axis=0, keepdims=True)


def _state_step(state, a, b, q, v, *, gb=32):
    g, dk, dv = state.shape
    tr = lambda x: x.reshape(g // gb, gb, dk).transpose(0, 2, 1)
    return pl.pallas_call(
        functools.partial(_state_step_kernel, gb=gb),
        grid=(g // gb,),
        in_specs=[pl.BlockSpec((gb, dk, dv), lambda i: (i, 0, 0)),
                  pl.BlockSpec((1, dk, gb), lambda i: (i, 0, 0)),
                  pl.BlockSpec((1, dk, gb), lambda i: (i, 0, 0)),
                  pl.BlockSpec((1, dk, gb), lambda i: (i, 0, 0)),
                  pl.BlockSpec((gb, dv), lambda i: (i, 0))],
        out_specs=[pl.BlockSpec((gb, dk, dv), lambda i: (i, 0, 0)),
                   pl.BlockSpec((gb, dv), lambda i: (i, 0))],
        out_shape=[jax.ShapeDtypeStruct((g, dk, dv), F32),
                   jax.ShapeDtypeStruct((g, dv), F32)],
        compiler_params=_params("parallel"),
        name="state_step",
    )(state, tr(a), tr(b), tr(q), v)


PAGES_PER_STEP = 4


def _fox_sample_kernel(pt_ref, q_ref, kn_ref, vn_ref, bias_ref, *rest):
    npg = PAGES_PER_STEP
    k_refs = rest[:npg]
    v_refs = rest[npg:2 * npg]
    o_ref = rest[2 * npg]
    m_sc, l_sc, acc_sc = rest[2 * npg + 1:]
    j = pl.program_id(1)

    @pl.when(j == 0)
    def _():
        m_sc[...] = jnp.full(m_sc.shape, NEG, F32)
        l_sc[...] = jnp.zeros_like(l_sc)
        acc_sc[...] = jnp.zeros_like(acc_sc)

    q = q_ref[0] * (F_HD ** -0.5)
    lane = lax.broadcasted_iota(jnp.int32, (F_HEADS, PAGE_SIZE), 1)
    tiles = []
    for pg in range(npg):
        raw = jnp.zeros((F_HEADS, PAGE_SIZE), F32)
        for pos in range(PAGE_SIZE):
            sp = jnp.sum(k_refs[pg][0, 0, pos] * q, axis=-1, keepdims=True)
            raw = jnp.where(lane == pos, sp, raw)
        tiles.append(raw + bias_ref[0, :, pg * PAGE_SIZE:(pg + 1) * PAGE_SIZE])

    m_prev = m_sc[:, 0:1]
    m_new = m_prev
    for tl in tiles:
        m_new = jnp.maximum(m_new, jnp.max(tl, axis=-1, keepdims=True))
    alpha = jnp.exp(m_prev - m_new)
    acc = acc_sc[...] * alpha
    l_new = l_sc[:, 0:1] * alpha
    for pg in range(npg):
        p = jnp.exp(tiles[pg] - m_new)
        l_new = l_new + jnp.sum(p, axis=-1, keepdims=True)
        for pos in range(PAGE_SIZE):
            acc = acc + p[:, pos:pos + 1] * v_refs[pg][0, 0, pos]
    acc_sc[...] = acc
    m_sc[...] = jnp.broadcast_to(m_new, m_sc.shape)
    l_sc[...] = jnp.broadcast_to(l_new, l_sc.shape)

    @pl.when(j == pl.num_programs(1) - 1)
    def _():
        s_new = jnp.sum(q * kn_ref[0], axis=-1, keepdims=True)
        m_f = jnp.maximum(m_new, s_new)
        a_old = jnp.exp(m_new - m_f)
        p_new = jnp.exp(s_new - m_f)
        o_ref[0] = (acc * a_old + p_new * vn_ref[0]) / (l_new * a_old + p_new)


def _fox_sample(layer, page_table, cache_k, cache_v, q, kn, vn, bias):
    db, n_pages = page_table.shape
    npg = PAGES_PER_STEP
    page_block = (1, 1, PAGE_SIZE, F_HEADS, F_HD)

    def page_spec(pg):
        return pl.BlockSpec(page_block, lambda i, j, pt: (layer, pt[i, j * npg + pg], 0, 0, 0))

    tok = pl.BlockSpec((1, F_HEADS, F_HD), lambda i, j, pt: (i, 0, 0))
    grid_spec = pltpu.PrefetchScalarGridSpec(
        num_scalar_prefetch=1,
        grid=(db, n_pages // npg),
        in_specs=[tok, tok, tok,
                  pl.BlockSpec((1, F_HEADS, npg * PAGE_SIZE), lambda i, j, pt: (i, 0, j))]
        + [page_spec(pg) for pg in range(npg)] + [page_spec(pg) for pg in range(npg)],
        out_specs=tok,
        scratch_shapes=[pltpu.VMEM((F_HEADS, LANES), F32), pltpu.VMEM((F_HEADS, LANES), F32),
                        pltpu.VMEM((F_HEADS, F_HD), F32)])
    return pl.pallas_call(
        _fox_sample_kernel,
        grid_spec=grid_spec,
        out_shape=jax.ShapeDtypeStruct((db, F_HEADS, F_HD), F32),
        compiler_params=_params("parallel", "arbitrary"),
        name="fox_sample",
    )(page_table, q, kn, vn, bias, *([cache_k] * npg), *([cache_v] * npg))


def _suffix_kernel(x_ref, m_ref, o_ref):
    n_pg = x_ref.shape[1] // PAGE_SIZE
    carry = jnp.zeros((x_ref.shape[0], 1), F32)
    for pg in reversed(range(n_pg)):
        x = x_ref[:, pg * PAGE_SIZE:(pg + 1) * PAGE_SIZE]
        o_ref[:, pg * PAGE_SIZE:(pg + 1) * PAGE_SIZE] = (
            jnp.dot(x, m_ref[...], precision=HIGHEST, preferred_element_type=F32) + carry)
        carry = carry + jnp.sum(x, axis=-1, keepdims=True)


def _suffix_sums(x, *, tr=256):
    r, p = x.shape
    msuf = jnp.asarray(np.tril(np.ones((PAGE_SIZE, PAGE_SIZE), np.float32), -1))
    return pl.pallas_call(
        _suffix_kernel,
        grid=(r // tr,),
        in_specs=[pl.BlockSpec((tr, p), lambda i: (i, 0)),
                  pl.BlockSpec((PAGE_SIZE, PAGE_SIZE), lambda i: (0, 0))],
        out_specs=pl.BlockSpec((tr, p), lambda i: (i, 0)),
        out_shape=jax.ShapeDtypeStruct((r, p), F32),
        compiler_params=_params("parallel"),
        name="suffix_sums",
    )(x, msuf)


def _rms(x, g):
    return x * lax.rsqrt(jnp.mean(x * x, axis=-1, keepdims=True) + EPS) * g


def _head_rms(o, g, heads):
    shp = o.shape
    o = o.reshape(shp[:-1] + (heads, shp[-1] // heads))
    y = o * lax.rsqrt(jnp.mean(o * o, axis=-1, keepdims=True) + EPS)
    return y.reshape(shp) * g


def _prep_layer(l, prm):
    w_in = prm['w_in'][l]
    seg = lambda i: w_in[:, _OFF[i]:_OFF[i + 1]]
    main = jnp.concatenate([seg(0), seg(1), seg(2), seg(5), seg(6), seg(7), seg(9), seg(10), seg(11), seg(12),
                            seg(13), seg(14), seg(15)], axis=1).astype(BF16)
    small = jnp.concatenate([seg(3), seg(4), seg(8), jnp.zeros((D_MODEL, LANES - 16), F32)], axis=1).astype(BF16)
    gate_bias = jnp.concatenate([prm['m_gate_b'][l], prm['f_gate_b'][l], jnp.zeros((LANES - 16,), F32)])[None, :]
    pl_sm = jax.nn.softmax(prm['h_lb_logits'].astype(F32), axis=0)
    cum = jnp.cumsum(pl_sm, axis=0)
    lb = (cum[l] - cum[0])[None, :]
    return dict(
        w_main=main, w_small=small, gate_bias=gate_bias, lb=lb,
        w_ada=prm['w_ada'][l].astype(BF16), b_ada=prm['b_ada'][l], nw=prm['norm_w'][l],
        cw=prm['m_conv_w'][l], cb=prm['m_conv_b'][l][None, :],
        wq=prm['w_mq'][l].astype(BF16), wk=prm['w_mk'][l].astype(BF16),
        hnorm_m=prm['m_hnorm'][l][None, :], skip=prm['m_skip'][l][None, :],
        hnorm_h=prm['h_hnorm'][l][None, :],
        w_pa=prm['w_pa'][l].astype(BF16), w_pb=prm['w_pb'][l].astype(BF16), w_pc=prm['w_pc'][l].astype(BF16),
        w_out=prm['w_out'][l].astype(BF16),
        w_gu=jnp.concatenate([prm['w_ffn_g'][l], prm['w_ffn_u'][l]], axis=1).astype(BF16),
        fcw=prm['ffn_conv_w'][l], fcb=prm['ffn_conv_b'][l], w_d=prm['w_ffn_d'][l].astype(BF16))


def _prompt_layer(x, ada, w):
    b, l, d = x.shape
    m = b * l
    tm = 1024
    sh1, sc1, g1, sh2, sc2, g2 = jnp.split(ada[:, None, :], 6, axis=-1)
    x2 = x.reshape(m, d)
    mul1 = w['nw'][0] * (1.0 + sc1)
    z = _pmm(x2, mul1, sh1, w['w_main'], tm=tm, tn=1024, tiles_per_group=l // tm).reshape(b, l, N_MAIN)
    zs = _pmm(x2, mul1, sh1, w['w_small'], tm=tm, tn=LANES, tiles_per_group=l // tm).reshape(b, l, LANES)

    gcol, cum = _gates(zs, w['gate_bias'], tc=512)
    grow = jnp.swapaxes(gcol[:, :, 0:8], 1, 2)
    fkt = jnp.swapaxes(cum[:, :, LANE_FF:LANE_FF + 8], 1, 2)

    hm, c1, n1, m1 = _mlstm_prompt(z, gcol, grow, w['cw'], w['cb'], w['wq'], w['wk'], w['hnorm_m'], w['skip'], c=128)
    att = _fox_prompt(z, cum, fkt, t=512)
    oh, s1 = _hgrn_prompt(z, w['lb'], w['hnorm_h'])

    ya = _mm(hm.reshape(m, -1), w['w_pa'], tm=tm, tn=1024)
    yb = _mm(att.reshape(m, -1), w['w_pb'], tm=tm, tn=1024)
    yc = _mm(oh.reshape(m, -1), w['w_pc'], tm=tm, tn=1024)
    z2 = z.reshape(m, N_MAIN)
    merged = (jax.nn.sigmoid(z2[:, COL_GA:COL_GA + d]) * ya + jax.nn.sigmoid(z2[:, COL_GB:COL_GB + d]) * yb
              + jax.nn.sigmoid(z2[:, COL_GC:COL_GC + d]) * yc)
    mix = _mm(merged, w['w_out'], tm=tm, tn=1024).reshape(b, l, d)
    x = x + (1.0 + g1) * _rms(mix, w['nw'][1])

    mul2 = w['nw'][2] * (1.0 + sc2)
    gu = _pmm(x.reshape(m, d), mul2, sh2, w['w_gu'], tm=tm, tn=512, tiles_per_group=l // tm).reshape(b, l, 2 * D_FF)
    g, u = gu[..., :D_FF], gu[..., D_FF:]
    gp = jnp.pad(g, ((0, 0), (FFN_CONV - 1, 0), (0, 0)))
    fcw = w['fcw']
    gc = w['fcb'] + fcw[0] * gp[:, 0:l] + fcw[1] * gp[:, 1:l + 1] + fcw[2] * gp[:, 2:l + 2]
    ffo = _mm((jax.nn.gelu(gc) * u).reshape(m, D_FF), w['w_d'], tm=512, tn=1024).reshape(b, l, d)
    x = x + (1.0 + g2) * _rms(ffo, w['nw'][3])

    k = z[..., BLK_FK * 512:(BLK_FK + 1) * 512].reshape(b, l, F_HEADS, F_HD)
    v = z[..., BLK_FV * 512:(BLK_FV + 1) * 512].reshape(b, l, F_HEADS, F_HD)
    lff = gcol[..., LANE_FF:LANE_FF + F_HEADS]
    mconv = z[:, l - (M_CONV - 1):, 0:BRANCH_WIDTH]
    fconv = g[:, l - (FFN_CONV - 1):, :]
    st = (k, v, lff, c1, n1[:, :M_HEADS, :], m1[:, :M_HEADS, 0], mconv, s1, fconv)
    return x, st


def _sample_layer(l, x, ada, w, page_table, cache_k, cache_v, cache_logf, st_c, st_n, st_m, st_mconv, st_h, st_fconv):
    db, d = x.shape
    sh1, sc1, g1, sh2, sc2, g2 = jnp.split(ada, 6, axis=-1)
    mul1 = (w['nw'][0] * (1.0 + sc1))[None]
    z = _pmm(x, mul1, sh1[None], w['w_main'], tm=db, tn=1024, tiles_per_group=1)
    zs = _pmm(x, mul1, sh1[None], w['w_small'], tm=db, tn=LANES, tiles_per_group=1) + w['gate_bias']
    blk = lambda i: z[:, i * 512:(i + 1) * 512]

    mx = blk(BLK_MX)
    cw = w['cw']
    mcv = w['cb'] + cw[0] * st_mconv[:, 0] + cw[1] * st_mconv[:, 1] + cw[2] * st_mconv[:, 2] + cw[3] * mx
    mconv_new = jnp.concatenate([st_mconv[:, 1:], mx[:, None, :]], axis=1)
    ca = jax.nn.silu(mcv)
    mq = _headmm(ca, w['wq']).reshape(db, M_HEADS, M_DK) * (M_DK ** -0.5)
    mk = _headmm(ca, w['wk']).reshape(db, M_HEADS, M_DK)
    ig = zs[:, LANE_MI:LANE_MI + M_HEADS]
    lfm = jax.nn.log_sigmoid(zs[:, LANE_MF:LANE_MF + M_HEADS])
    mn = jnp.maximum(ig, lfm + st_m)
    wk_ = jnp.exp(ig - mn)[..., None]
    dec = jnp.exp(lfm + st_m - mn)[..., None]
    g_ = db * M_HEADS
    c_new, onum = _state_step(st_c.reshape(g_, M_DK, M_DK),
                              jnp.broadcast_to(dec, (db, M_HEADS, M_DK)).reshape(g_, M_DK),
                              (wk_ * mk).reshape(g_, M_DK), mq.reshape(g_, M_DK),
                              blk(BLK_MV).reshape(g_, M_DK))
    n_new = dec * st_n + wk_ * mk
    den = jnp.sum(mq * n_new, axis=-1, keepdims=True)
    hm = onum.reshape(db, M_HEADS, M_DK) / jnp.maximum(jnp.abs(den), jnp.exp(-mn)[..., None])
    hm = (_head_rms(hm.reshape(db, -1), w['hnorm_m'], M_HEADS) + w['skip'] * ca) * jax.nn.silu(blk(BLK_MO))
    ya = _mm(hm, w['w_pa'], tm=db, tn=1024)

    fq = blk(BLK_FQ).reshape(db, F_HEADS, F_HD)
    fk = blk(BLK_FK).reshape(db, F_HEADS, F_HD)
    fv = blk(BLK_FV).reshape(db, F_HEADS, F_HD)
    lff = jax.nn.log_sigmoid(zs[:, LANE_FF:LANE_FF + F_HEADS])
    lfp = cache_logf[l][page_table].reshape(db, -1, F_HEADS)
    lfpt = jnp.swapaxes(lfp, 1, 2).reshape(db * F_HEADS, -1)
    bias = _suffix_sums(lfpt).reshape(db, F_HEADS, -1) + lff[:, :, None]
    att = _fox_sample(l, page_table, cache_k, cache_v, fq, fk, fv, bias)
    yb = _mm(att.reshape(db, -1), w['w_pb'], tm=db, tn=1024)

    hf = blk(BLK_HF)
    lb = w['lb']
    f = lb + (1.0 - lb) * jax.nn.sigmoid(hf)
    hk = (1.0 - lb) * jax.nn.sigmoid(-hf)
    hq = jax.nn.silu(blk(BLK_HQ))
    g2_ = db * H_HEADS
    s_new, oh = _state_step(st_h.reshape(g2_, H_DK, H_DK), f.reshape(g2_, H_DK), hk.reshape(g2_, H_DK),
                            hq.reshape(g2_, H_DK), blk(BLK_HI).reshape(g2_, H_DK))
    oh = _head_rms(oh.reshape(db, -1), w['hnorm_h'], H_HEADS) * jax.nn.silu(blk(BLK_HG))
    yc = _mm(oh, w['w_pc'], tm=db, tn=1024)

    merged = (jax.nn.sigmoid(z[:, COL_GA:COL_GA + d]) * ya + jax.nn.sigmoid(z[:, COL_GB:COL_GB + d]) * yb
              + jax.nn.sigmoid(z[:, COL_GC:COL_GC + d]) * yc)
    mix = _mm(merged, w['w_out'], tm=db, tn=1024)
    x = x + (1.0 + g1) * _rms(mix, w['nw'][1])

    mul2 = (w['nw'][2] * (1.0 + sc2))[None]
    gu = _pmm(x, mul2, sh2[None], w['w_gu'], tm=db, tn=512, tiles_per_group=1)
    g, u = gu[:, :D_FF], gu[:, D_FF:]
    fcw = w['fcw']
    gc = w['fcb'] + fcw[0] * st_fconv[:, 0] + fcw[1] * st_fconv[:, 1] + fcw[2] * g
    fconv_new = jnp.concatenate([st_fconv[:, 1:], g[:, None, :]], axis=1)
    ffo = _mm(jax.nn.gelu(gc) * u, w['w_d'], tm=db, tn=1024)
    x = x + (1.0 + g2) * _rms(ffo, w['nw'][3])

    st = (fk[:, None], fv[:, None], lff[:, None], c_new.reshape(db, M_HEADS, M_DK, M_DK), n_new, mn,
          mconv_new, s_new.reshape(db, H_HEADS, H_DK, H_DK), fconv_new)
    return x, st


def kernel(x_prompt, x_sample, c_prompt, c_sample, cache_k, cache_v, cache_logf, page_table, state_mlstm_c, state_mlstm_n, state_mlstm_m, state_mlstm_conv, state_hgrn, state_ffn_conv, norm_w, w_ada, b_ada, w_in, m_conv_w, m_conv_b, w_mq, w_mk, m_gate_b, m_skip, m_hnorm, f_gate_b, h_lb_logits, h_hnorm, w_pa, w_pb, w_pc, w_out, w_ffn_g, w_ffn_u, ffn_conv_w, ffn_conv_b, w_ffn_d):
    prm = dict(norm_w=norm_w, w_ada=w_ada, b_ada=b_ada, w_in=w_in, m_conv_w=m_conv_w, m_conv_b=m_conv_b,
               w_mq=w_mq, w_mk=w_mk, m_gate_b=m_gate_b, m_skip=m_skip, m_hnorm=m_hnorm, f_gate_b=f_gate_b,
               h_lb_logits=h_lb_logits, h_hnorm=h_hnorm, w_pa=w_pa, w_pb=w_pb, w_pc=w_pc, w_out=w_out,
               w_ffn_g=w_ffn_g, w_ffn_u=w_ffn_u, ffn_conv_w=ffn_conv_w, ffn_conv_b=ffn_conv_b, w_ffn_d=w_ffn_d)
    nb = x_prompt.shape[0]
    db = x_sample.shape[0]
    pad = (-(nb + db)) % 16
    c_all = jnp.concatenate([c_sample, c_prompt, jnp.zeros((pad, D_MODEL), F32)], axis=0)
    sc_all = jax.nn.silu(c_all)

    xp = x_prompt
    xs = x_sample[:, 0, :]
    pst = [[] for _ in range(9)]
    sst = [[] for _ in range(9)]
    for l in range(DEPTH):
        w = _prep_layer(l, prm)
        ada = _mm(sc_all, w['w_ada'], tm=sc_all.shape[0], tn=1024) + w['b_ada']
        xp, st = _prompt_layer(xp, ada[db:db + nb], w)
        for j in range(9):
            pst[j].append(st[j])
        xs, st = _sample_layer(l, xs, ada[:db], w, page_table, cache_k, cache_v, cache_logf,
                               state_mlstm_c[l], state_mlstm_n[l], state_mlstm_m[l], state_mlstm_conv[l],
                               state_hgrn[l], state_ffn_conv[l])
        for j in range(9):
            sst[j].append(st[j])
    p_out = [jnp.stack(s, axis=0) for s in pst]
    s_out = [jnp.stack(s, axis=0) for s in sst]
    return (xp, xs[:, None, :], *p_out, *s_out)
```

```python
import functools
import math

import numpy as np
import jax
import jax.numpy as jnp
from jax import lax
from jax.experimental import pallas as pl
from jax.experimental.pallas import tpu as pltpu

F32 = jnp.float32
BF16 = jnp.bfloat16
HIGHEST = lax.Precision.HIGHEST

D_MODEL = 1024
DEPTH = 4
PAGE_SIZE = 128
BRANCH_WIDTH = D_MODEL // 2
M_HEADS = 4
M_DK = BRANCH_WIDTH // M_HEADS
M_CONV = 4
F_HEADS = 8
F_HD = BRANCH_WIDTH // F_HEADS
H_HEADS = 4
H_DK = BRANCH_WIDTH // H_HEADS
D_FF = ((8 * D_MODEL // 3 + 127) // 128) * 128
FFN_CONV = 3
EPS = 1e-6
NEG = -1e30

LANES = 128
VMEM_LIMIT = 56 * 1024 * 1024

_SIZES = (512, 512, 512, 4, 4, 512, 512, 512, 8, 512, 512, 512, 512, 1024, 1024, 1024)
_OFF = tuple(int(v) for v in np.concatenate([[0], np.cumsum(_SIZES)]))
BLK_MX, BLK_MV, BLK_MO, BLK_FQ, BLK_FK, BLK_FV, BLK_HQ, BLK_HF, BLK_HI, BLK_HG = range(10)
COL_GA, COL_GB, COL_GC = 5120, 6144, 7168
N_MAIN = 8192
LANE_MI, LANE_MF, LANE_FF = 0, 4, 8


def _nt(a, b):
    return lax.dot_general(a, b, (((1,), (1,)), ((), ())), preferred_element_type=F32)


def _dot(a, b):
    return jnp.dot(a, b, preferred_element_type=F32)


def _log_sigmoid(x):
    return jnp.minimum(x, 0.0) - jnp.log1p(jnp.exp(-jnp.abs(x)))


def _sigmoid(x):
    return 1.0 / (1.0 + jnp.exp(-x))


def _params(*sem):
    return pltpu.CompilerParams(dimension_semantics=sem, vmem_limit_bytes=VMEM_LIMIT)


def _mm_kernel(x_ref, w_ref, o_ref, xs):
    @pl.when(pl.program_id(1) == 0)
    def _():
        xs[...] = x_ref[...].astype(BF16)

    o_ref[...] = _dot(xs[...], w_ref[...])


def _mm(x, w, *, tm, tn):
    m, k = x.shape
    n = w.shape[1]
    return pl.pallas_call(
        _mm_kernel,
        grid=(m // tm, n // tn),
        in_specs=[pl.BlockSpec((tm, k), lambda i, j: (i, 0)),
                  pl.BlockSpec((k, tn), lambda i, j: (0, j))],
        out_specs=pl.BlockSpec((tm, tn), lambda i, j: (i, j)),
        out_shape=jax.ShapeDtypeStruct((m, n), F32),
        scratch_shapes=[pltpu.VMEM((tm, k), BF16)],
        compiler_params=_params("parallel", "arbitrary"),
        name="mm",
    )(x, w)


def _pmm_kernel(x_ref, mul_ref, add_ref, w_ref, o_ref, hs):
    @pl.when(pl.program_id(1) == 0)
    def _():
        x = x_ref[...]
        y = x * lax.rsqrt(jnp.mean(x * x, axis=-1, keepdims=True) + EPS)
        hs[...] = (y * mul_ref[0] + add_ref[0]).astype(BF16)

    o_ref[...] = _dot(hs[...], w_ref[...])


def _pmm(x, mul, add, w, *, tm, tn, tiles_per_group):
    m, k = x.shape
    n = w.shape[1]
    r = mul.shape[1]
    return pl.pallas_call(
        _pmm_kernel,
        grid=(m // tm, n // tn),
        in_specs=[pl.BlockSpec((tm, k), lambda i, j: (i, 0)),
                  pl.BlockSpec((1, r, k), lambda i, j: (i // tiles_per_group, 0, 0)),
                  pl.BlockSpec((1, r, k), lambda i, j: (i // tiles_per_group, 0, 0)),
                  pl.BlockSpec((k, tn), lambda i, j: (0, j))],
        out_specs=pl.BlockSpec((tm, tn), lambda i, j: (i, j)),
        out_shape=jax.ShapeDtypeStruct((m, n), F32),
        scratch_shapes=[pltpu.VMEM((tm, k), BF16)],
        compiler_params=_params("parallel", "arbitrary"),
        name="pmm",
    )(x, mul, add, w)


def _headmm_kernel(x_ref, w_ref, o_ref):
    o_ref[...] = _dot(x_ref[...].astype(BF16), w_ref[0])


def _headmm(x, w):
    m = x.shape[0]
    h, d, _ = w.shape
    return pl.pallas_call(
        _headmm_kernel,
        grid=(h,),
        in_specs=[pl.BlockSpec((m, d), lambda i: (0, i)),
                  pl.BlockSpec((1, d, d), lambda i: (i, 0, 0))],
        out_specs=pl.BlockSpec((m, d), lambda i: (0, i)),
        out_shape=jax.ShapeDtypeStruct((m, h * d), F32),
        compiler_params=_params("parallel"),
        name="headmm",
    )(x, w)


def _gates_kernel(zs_ref, bias_ref, mcum_ref, g_ref, cum_ref, carry):
    @pl.when(pl.program_id(1) == 0)
    def _():
        carry[...] = jnp.zeros_like(carry)

    z = zs_ref[0] + bias_ref[...]
    ls = _log_sigmoid(z)
    lane = lax.broadcasted_iota(jnp.int32, z.shape, 1)
    g_ref[0] = jnp.where(lane < LANE_MF, z, ls)
    cum = jnp.dot(mcum_ref[...], ls, precision=HIGHEST, preferred_element_type=F32) + carry[0:1, :]
    cum_ref[0] = cum
    tc = z.shape[0]
    carry[...] = jnp.broadcast_to(cum[tc - 1:tc, :], carry.shape)


def _gates(zs, bias, *, tc):
    b, l, _ = zs.shape
    mcum = jnp.asarray(np.tril(np.ones((tc, tc), np.float32)))
    return pl.pallas_call(
        _gates_kernel,
        grid=(b, l // tc),
        in_specs=[pl.BlockSpec((1, tc, LANES), lambda i, c: (i, c, 0)),
                  pl.BlockSpec((1, LANES), lambda i, c: (0, 0)),
                  pl.BlockSpec((tc, tc), lambda i, c: (0, 0))],
        out_specs=[pl.BlockSpec((1, tc, LANES), lambda i, c: (i, c, 0)),
                   pl.BlockSpec((1, tc, LANES), lambda i, c: (i, c, 0))],
        out_shape=[jax.ShapeDtypeStruct((b, l, LANES), F32),
                   jax.ShapeDtypeStruct((b, l, LANES), F32)],
        scratch_shapes=[pltpu.VMEM((8, LANES), F32)],
        compiler_params=_params("parallel", "arbitrary"),
        name="gates",
    )(zs, bias, mcum)


def _mlstm_kernel(mx_ref, mv_ref, mo_ref, gc_ref, gr_ref, cw_ref, cb_ref, wq_ref, wk_ref,
                  hn_ref, sk_ref, mcum_ref,
                  out_ref, cfin_ref, nfin_ref, mfin_ref,
                  xbuf, c_sc, n_sc, m_sc, *, c):
    ci = pl.program_id(1)

    @pl.when(ci == 0)
    def _():
        xbuf[0:8, :] = jnp.zeros((8, BRANCH_WIDTH), F32)
        c_sc[...] = jnp.zeros_like(c_sc)
        n_sc[...] = jnp.zeros_like(n_sc)
        m_sc[...] = jnp.zeros_like(m_sc)

    x = mx_ref[0]
    xbuf[8:8 + c, :] = x
    cw = cw_ref[...]
    mcv = (cb_ref[...] + cw[3:4, :] * x + cw[2:3, :] * xbuf[7:7 + c, :]
           + cw[1:2, :] * xbuf[6:6 + c, :] + cw[0:1, :] * xbuf[5:5 + c, :])
    xbuf[0:8, :] = x[c - 8:c, :]
    ca = mcv * _sigmoid(mcv)

    gcol = gc_ref[0]
    grow = gr_ref[0]
    mcum = mcum_ref[...]
    fcol = jnp.dot(mcum, gcol, precision=HIGHEST, preferred_element_type=F32)
    frow = lax.dot_general(grow, mcum, (((1,), (1,)), ((), ())), precision=HIGHEST,
                           preferred_element_type=F32)
    tri = (lax.broadcasted_iota(jnp.int32, (c, c), 0) >= lax.broadcasted_iota(jnp.int32, (c, c), 1))
    scale = M_DK ** -0.5
    mv = mv_ref[0]
    mo = mo_ref[0]

    for h in range(M_HEADS):
        sl = slice(h * M_DK, (h + 1) * M_DK)
        cah = ca[:, sl]
        cb16 = cah.astype(BF16)
        q = _dot(cb16, wq_ref[h]) * scale
        k = _dot(cb16, wk_ref[h])
        v = mv[:, sl]
        q16 = q.astype(BF16)
        v16 = v.astype(BF16)
        f_c = fcol[:, LANE_MF + h:LANE_MF + h + 1]
        ig_c = gcol[:, h:h + 1]
        f_r = frow[LANE_MF + h:LANE_MF + h + 1, :]
        ig_r = grow[h:h + 1, :]
        m_prev = m_sc[h:h + 1, 0:1]

        d = jnp.where(tri, f_c - f_r + ig_r, NEG)
        inter = f_c + m_prev
        mt = jnp.maximum(jnp.max(d, axis=-1, keepdims=True), inter)
        a = _nt(q16, k.astype(BF16)) * jnp.exp(d - mt)
        ei = jnp.exp(inter - mt)
        num = _dot(a.astype(BF16), v16) + ei * _dot(q16, c_sc[h].astype(BF16))
        den = jnp.sum(a, axis=-1, keepdims=True) + ei * jnp.sum(q * n_sc[h:h + 1, :], axis=-1, keepdims=True)
        hh = num / jnp.maximum(jnp.abs(den), jnp.exp(-mt))

        fl = f_c[c - 1:c, :]
        wl_c = fl - f_c + ig_c
        wl_r = fl - f_r + ig_r
        mn = jnp.maximum(fl + m_prev, jnp.max(wl_r, axis=-1, keepdims=True))
        dec = jnp.exp(fl + m_prev - mn)
        kw = k * jnp.exp(wl_c - mn)
        c_sc[h] = dec * c_sc[h] + _dot(kw.T.astype(BF16), v16)
        n_sc[h:h + 1, :] = dec * n_sc[h:h + 1, :] + jnp.sum(kw, axis=0, keepdims=True)
        m_sc[h:h + 1, :] = jnp.broadcast_to(mn, (1, LANES))

        hn = hh * lax.rsqrt(jnp.mean(hh * hh, axis=-1, keepdims=True) + EPS)
        hn = hn * hn_ref[:, sl] + sk_ref[:, sl] * cah
        moh = mo[:, sl]
        out_ref[0, :, sl] = hn * (moh * _sigmoid(moh))

    @pl.when(ci == pl.num_programs(1) - 1)
    def _():
        cfin_ref[0] = c_sc[...]
        nfin_ref[0] = n_sc[...]
        mfin_ref[0] = m_sc[...]


def _mlstm_prompt(z, gcol, grow, cw, cb, wq, wk, hnorm, skip, *, c):
    b, l, _ = z.shape
    mcum = jnp.asarray(np.tril(np.ones((c, c), np.float32)))
    w = BRANCH_WIDTH
    full = lambda shape: pl.BlockSpec(shape, lambda i, j: (0,) * len(shape))
    return pl.pallas_call(
        functools.partial(_mlstm_kernel, c=c),
        grid=(b, l // c),
        in_specs=[pl.BlockSpec((1, c, w), lambda i, j: (i, j, BLK_MX)),
                  pl.BlockSpec((1, c, w), lambda i, j: (i, j, BLK_MV)),
                  pl.BlockSpec((1, c, w), lambda i, j: (i, j, BLK_MO)),
                  pl.BlockSpec((1, c, LANES), lambda i, j: (i, j, 0)),
                  pl.BlockSpec((1, 8, c), lambda i, j: (i, 0, j)),
                  full((M_CONV, w)), full((1, w)),
                  full((M_HEADS, M_DK, M_DK)), full((M_HEADS, M_DK, M_DK)),
                  full((1, w)), full((1, w)), full((c, c))],
        out_specs=[pl.BlockSpec((1, c, w), lambda i, j: (i, j, 0)),
                   pl.BlockSpec((1, M_HEADS, M_DK, M_DK), lambda i, j: (i, 0, 0, 0)),
                   pl.BlockSpec((1, 8, LANES), lambda i, j: (i, 0, 0)),
                   pl.BlockSpec((1, 8, LANES), lambda i, j: (i, 0, 0))],
        out_shape=[jax.ShapeDtypeStruct((b, l, w), F32),
                   jax.ShapeDtypeStruct((b, M_HEADS, M_DK, M_DK), F32),
                   jax.ShapeDtypeStruct((b, 8, LANES), F32),
                   jax.ShapeDtypeStruct((b, 8, LANES), F32)],
        scratch_shapes=[pltpu.VMEM((c + 8, w), F32),
                        pltpu.VMEM((M_HEADS, M_DK, M_DK), F32),
                        pltpu.VMEM((8, LANES), F32),
                        pltpu.VMEM((8, LANES), F32)],
        compiler_params=_params("parallel", "arbitrary"),
        name="mlstm_prompt",
    )(z, z, z, gcol, grow, cw, cb, wq, wk, hnorm, skip, mcum)


def _fox_step(masked, q_ref, k_ref, v_ref, fq_ref, fk_ref, m_sc, l_sc, acc_sc, tq, tk):
    q = q_ref[0] * (F_HD ** -0.5)
    k = k_ref[0].astype(BF16)
    v = v_ref[0].astype(BF16)
    fq = fq_ref[0]
    fk = fk_ref[0]
    lo = lax.broadcasted_iota(jnp.int32, (tq, LANES), 1) < F_HD
    if masked:
        tri = (lax.broadcasted_iota(jnp.int32, (tq, tk), 0) >= lax.broadcasted_iota(jnp.int32, (tq, tk), 1))
    for pr in range(F_HEADS // 2):
        sl = slice(pr * LANES, (pr + 1) * LANES)
        q2, k2, v2 = q[:, sl], k[:, sl], v[:, sl]
        alphas, pvs = [], []
        for hh in range(2):
            h = 2 * pr + hh
            qm = jnp.where(lo if hh == 0 else jnp.logical_not(lo), q2, 0.0).astype(BF16)
            s = _nt(qm, k2) + fq[:, LANE_FF + h:LANE_FF + h + 1] - fk[h:h + 1, :]
            if masked:
                s = jnp.where(tri, s, NEG)
            m_prev = m_sc[:, h:h + 1]
            m_new = jnp.maximum(m_prev, jnp.max(s, axis=-1, keepdims=True))
            alpha = jnp.exp(m_prev - m_new)
            p = jnp.exp(s - m_new)
            l_sc[:, h:h + 1] = alpha * l_sc[:, h:h + 1] + jnp.sum(p, axis=-1, keepdims=True)
            m_sc[:, h:h + 1] = m_new
            pvs.append(_dot(p.astype(BF16), v2))
            alphas.append(alpha)
        acc_sc[pr] = acc_sc[pr] * jnp.where(lo, alphas[0], alphas[1]) + jnp.where(lo, pvs[0], pvs[1])


def _fox_kernel(qi_ref, kj_ref, q_ref, k_ref, v_ref, fq_ref, fk_ref, o_ref, m_sc, l_sc, acc_sc, *, tq, tk):
    s = pl.program_id(1)
    i = qi_ref[s]
    j = kj_ref[s]

    @pl.when(j == 0)
    def _():
        m_sc[...] = jnp.full(m_sc.shape, NEG, F32)
        l_sc[...] = jnp.zeros_like(l_sc)
        acc_sc[...] = jnp.zeros_like(acc_sc)

    @pl.when(j < i)
    def _():
        _fox_step(False, q_ref, k_ref, v_ref, fq_ref, fk_ref, m_sc, l_sc, acc_sc, tq, tk)

    @pl.when(j == i)
    def _():
        _fox_step(True, q_ref, k_ref, v_ref, fq_ref, fk_ref, m_sc, l_sc, acc_sc, tq, tk)
        lo = lax.broadcasted_iota(jnp.int32, (tq, LANES), 1) < F_HD
        for pr in range(F_HEADS // 2):
            inv = jnp.where(lo, 1.0 / l_sc[:, 2 * pr:2 * pr + 1], 1.0 / l_sc[:, 2 * pr + 1:2 * pr + 2])
            o_ref[0, :, pr * LANES:(pr + 1) * LANES] = acc_sc[pr] * inv


def _fox_prompt(z, cum, fkt, *, t):
    b, l, _ = z.shape
    nb = l // t
    qi = np.array([i for i in range(nb) for _ in range(i + 1)], np.int32)
    kj = np.array([j for i in range(nb) for j in range(i + 1)], np.int32)
    w = BRANCH_WIDTH
    grid_spec = pltpu.PrefetchScalarGridSpec(
        num_scalar_prefetch=2,
        grid=(b, len(qi)),
        in_specs=[pl.BlockSpec((1, t, w), lambda i, s, qi, kj: (i, qi[s], BLK_FQ)),
                  pl.BlockSpec((1, t, w), lambda i, s, qi, kj: (i, kj[s], BLK_FK)),
                  pl.BlockSpec((1, t, w), lambda i, s, qi, kj: (i, kj[s], BLK_FV)),
                  pl.BlockSpec((1, t, LANES), lambda i, s, qi, kj: (i, qi[s], 0)),
                  pl.BlockSpec((1, 8, t), lambda i, s, qi, kj: (i, 0, kj[s]))],
        out_specs=pl.BlockSpec((1, t, w), lambda i, s, qi, kj: (i, qi[s], 0)),
        scratch_shapes=[pltpu.VMEM((t, LANES), F32), pltpu.VMEM((t, LANES), F32),
                        pltpu.VMEM((F_HEADS // 2, t, LANES), F32)])
    return pl.pallas_call(
        functools.partial(_fox_kernel, tq=t, tk=t),
        grid_spec=grid_spec,
        out_shape=jax.ShapeDtypeStruct((b, l, w), F32),
        compiler_params=_params("parallel", "arbitrary"),
        name="fox_prompt",
    )(jnp.asarray(qi), jnp.asarray(kj), z, z, z, cum, fkt)


HG_C = 128
HG_LEVELS = 7


def _hgrn_mats(c):
    mats = [np.tril(np.ones((c, c), np.float32))]
    for lev in range(HG_LEVELS):
        s = (c // 2) >> lev
        m = np.zeros((c, c), np.float32)
        for t in range(c):
            ref = (t // (2 * s)) * 2 * s + s - 1
            if t % (2 * s) >= s:
                m[t, ref + 1:t + 1] = 1.0
            else:
                m[t, t + 1:ref + 1] = 1.0
        mats.append(m)
    mats.append(np.triu(np.ones((c, c), np.float32), 1))
    return np.concatenate(mats, axis=0)


def _hgrn_kernel(hq_ref, hf_ref, hi_ref, hg_ref, lb_ref, hn_ref, mall_ref, o_ref, sfin_ref, st_sc):
    c = HG_C
    ci = pl.program_id(2)

    @pl.when(ci == 0)
    def _():
        st_sc[...] = jnp.zeros_like(st_sc)

    hf = hf_ref[0]
    lb = lb_ref[...]
    la = jnp.log(lb + (1.0 - lb) * _sigmoid(hf))
    k = (1.0 - lb) * _sigmoid(-hf)
    hq = hq_ref[0]
    q = hq * _sigmoid(hq)
    v16 = hi_ref[0].astype(BF16)

    args = jnp.dot(mall_ref[...], la, precision=HIGHEST, preferred_element_type=F32)
    b = args[0:c]
    ti = lax.broadcasted_iota(jnp.int32, (c, c), 0)
    si = lax.broadcasted_iota(jnp.int32, (c, c), 1)
    rowi = lax.broadcasted_iota(jnp.int32, (c, H_DK), 0)
    a = jnp.where(ti == si, _nt(q.astype(BF16), k.astype(BF16)), 0.0)
    for lev in range(HG_LEVELS):
        sh = int(math.log2(c // 2)) - lev
        upper = ((rowi >> sh) & 1) == 1
        x16 = (jnp.where(upper, q, k) * jnp.exp(args[(lev + 1) * c:(lev + 2) * c])).astype(BF16)
        p = _nt(x16, x16)
        mask = ((ti >> (sh + 1)) == (si >> (sh + 1))) & (((ti >> sh) & 1) == 1) & (((si >> sh) & 1) == 0)
        a = a + jnp.where(mask, p, 0.0)

    st = st_sc[...]
    o = _dot(a.astype(BF16), v16) + _nt((q * jnp.exp(b)).astype(BF16), st.astype(BF16))
    kt = k * jnp.exp(args[(HG_LEVELS + 1) * c:(HG_LEVELS + 2) * c])
    st_new = st * jnp.exp(b[c - 1:c, :]) + _dot(hi_ref[0].T.astype(BF16), kt.astype(BF16))
    st_sc[...] = st_new

    hg = hg_ref[0]
    on = o * lax.rsqrt(jnp.mean(o * o, axis=-1, keepdims=True) + EPS)
    o_ref[0] = on * hn_ref[...] * (hg * _sigmoid(hg))

    @pl.when(ci == pl.num_programs(2) - 1)
    def _():
        sfin_ref[0, 0] = st_new.T


def _hgrn_prompt(z, lb, hnorm):
    b, l, _ = z.shape
    c = HG_C
    mall = jnp.asarray(_hgrn_mats(c))
    blk = lambda base: pl.BlockSpec((1, c, H_DK), lambda i, h, j: (i, j, base * 4 + h))
    return pl.pallas_call(
        _hgrn_kernel,
        grid=(b, H_HEADS, l // c),
        in_specs=[blk(BLK_HQ), blk(BLK_HF), blk(BLK_HI), blk(BLK_HG),
                  pl.BlockSpec((1, H_DK), lambda i, h, j: (0, h)),
                  pl.BlockSpec((1, H_DK), lambda i, h, j: (0, h)),
                  pl.BlockSpec(mall.shape, lambda i, h, j: (0, 0))],
        out_specs=[pl.BlockSpec((1, c, H_DK), lambda i, h, j: (i, j, h)),
                   pl.BlockSpec((1, 1, H_DK, H_DK), lambda i, h, j: (i, h, 0, 0))],
        out_shape=[jax.ShapeDtypeStruct((b, l, BRANCH_WIDTH), F32),
                   jax.ShapeDtypeStruct((b, H_HEADS, H_DK, H_DK), F32)],
        scratch_shapes=[pltpu.VMEM((H_DK, H_DK), F32)],
        compiler_params=_params("parallel", "parallel", "arbitrary"),
        name="hgrn_prompt",
    )(z, z, z, z, lb, hnorm, mall)


def _state_step_kernel(st_ref, at_ref, bt_ref, qt_ref, v_ref, so_ref, o_ref, *, gb):
    at = at_ref[0]
    bt = bt_ref[0]
    qt = qt_ref[0]
    v = v_ref[...]
    for j in range(gb):
        s_new = st_ref[j] * at[:, j:j + 1] + bt[:, j:j + 1] * v[j:j + 1, :]
        so_ref[j] = s_new
        o_ref[j:j + 1, :] = jnp.sum(s_new * qt[:, j:j + 1], axis=0, keepdims=True)


def _state_step(state, a, b, q, v, *, gb=32):
    g, dk, dv = state.shape
    tr = lambda x: x.reshape(g // gb, gb, dk).transpose(0, 2, 1)
    return pl.pallas_call(
        functools.partial(_state_step_kernel, gb=gb),
        grid=(g // gb,),
        in_specs=[pl.BlockSpec((gb, dk, dv), lambda i: (i, 0, 0)),
                  pl.BlockSpec((1, dk, gb), lambda i: (i, 0, 0)),
                  pl.BlockSpec((1, dk, gb), lambda i: (i, 0, 0)),
                  pl.BlockSpec((1, dk, gb), lambda i: (i, 0, 0)),
                  pl.BlockSpec((gb, dv), lambda i: (i, 0))],
        out_specs=[pl.BlockSpec((gb, dk, dv), lambda i: (i, 0, 0)),
                   pl.BlockSpec((gb, dv), lambda i: (i, 0))],
        out_shape=[jax.ShapeDtypeStruct((g, dk, dv), F32),
                   jax.ShapeDtypeStruct((g, dv), F32)],
        compiler_params=_params("parallel"),
        name="state_step",
    )(state, tr(a), tr(b), tr(q), v)


def _fox_sample_kernel(pt_ref, q_ref, kn_ref, lfn_ref, msuf_ref, *rest, n_pages):
    k_refs = rest[:n_pages]
    v_refs = rest[n_pages:2 * n_pages]
    lf_refs = rest[2 * n_pages:3 * n_pages]
    ot_ref, st_ref = rest[3 * n_pages:]

    q = q_ref[0] * (F_HD ** -0.5)
    eye = (lax.broadcasted_iota(jnp.int32, (F_HD, F_HD), 0) == lax.broadcasted_iota(jnp.int32, (F_HD, F_HD), 1))
    ones = jnp.ones((F_HD, PAGE_SIZE), F32)
    qb = [jnp.dot(jnp.where(eye, q[h:h + 1, :], 0.0), ones, precision=HIGHEST, preferred_element_type=F32)
          for h in range(F_HEADS)]
    hrow = lax.broadcasted_iota(jnp.int32, (F_HEADS, PAGE_SIZE), 0)
    lfn = lfn_ref[0]

    carry = jnp.zeros((F_HEADS, 1), F32)
    tiles = [None] * n_pages
    for pg in reversed(range(n_pages)):
        lf = lf_refs[pg][0, 0]
        after = jnp.dot(lf, msuf_ref[...], precision=HIGHEST, preferred_element_type=F32) + carry
        carry = carry + jnp.sum(lf, axis=-1, keepdims=True)
        s = after + lfn
        for h in range(F_HEADS):
            row = jnp.sum(k_refs[pg][0, 0, h] * qb[h], axis=0, keepdims=True)
            s = s + jnp.where(hrow == h, row, 0.0)
        tiles[pg] = s

    s_new = jnp.sum(q * kn_ref[0], axis=-1, keepdims=True)
    m = s_new
    for t in tiles:
        m = jnp.maximum(m, jnp.max(t, axis=-1, keepdims=True))
    p_new = jnp.exp(s_new - m)
    l = p_new
    probs = []
    for t in tiles:
        p = jnp.exp(t - m)
        l = l + jnp.sum(p, axis=-1, keepdims=True)
        probs.append(p)

    lane = lax.broadcasted_iota(jnp.int32, (F_HD, LANES), 1)
    ot = jnp.zeros((F_HD, LANES), F32)
    for h in range(F_HEADS):
        acc = jnp.zeros((F_HD, PAGE_SIZE), F32)
        for pg in range(n_pages):
            acc = acc + v_refs[pg][0, 0, h] * probs[pg][h:h + 1, :]
        ot = jnp.where(lane == h, jnp.sum(acc, axis=-1, keepdims=True), ot)
    ot_ref[0] = ot
    slane = lax.broadcasted_iota(jnp.int32, (F_HEADS, LANES), 1)
    st_ref[0] = jnp.where(slane == 0, l, jnp.where(slane == 1, p_new, 0.0))


def _fox_sample(layer, page_table, cache_kt, cache_vt, cache_lft, q, kn, vn, lff):
    db, n_pages = page_table.shape
    msuf = jnp.asarray(np.tril(np.ones((PAGE_SIZE, PAGE_SIZE), np.float32), -1))
    lfn = jnp.broadcast_to(lff[:, :, None], (db, F_HEADS, LANES))

    def kv_spec(pg):
        return pl.BlockSpec((1, 1, F_HEADS, F_HD, PAGE_SIZE), lambda i, pt: (layer, pt[i, pg], 0, 0, 0))

    def lf_spec(pg):
        return pl.BlockSpec((1, 1, F_HEADS, PAGE_SIZE), lambda i, pt: (layer, pt[i, pg], 0, 0))

    tok = pl.BlockSpec((1, F_HEADS, F_HD), lambda i, pt: (i, 0, 0))
    grid_spec = pltpu.PrefetchScalarGridSpec(
        num_scalar_prefetch=1,
        grid=(db,),
        in_specs=[tok, tok,
                  pl.BlockSpec((1, F_HEADS, LANES), lambda i, pt: (i, 0, 0)),
                  pl.BlockSpec((PAGE_SIZE, PAGE_SIZE), lambda i, pt: (0, 0))]
        + [kv_spec(pg) for pg in range(n_pages)] + [kv_spec(pg) for pg in range(n_pages)]
        + [lf_spec(pg) for pg in range(n_pages)],
        out_specs=[pl.BlockSpec((1, F_HD, LANES), lambda i, pt: (i, 0, 0)),
                   pl.BlockSpec((1, F_HEADS, LANES), lambda i, pt: (i, 0, 0))])
    ot, stats = pl.pallas_call(
        functools.partial(_fox_sample_kernel, n_pages=n_pages),
        grid_spec=grid_spec,
        out_shape=[jax.ShapeDtypeStruct((db, F_HD, LANES), F32),
                   jax.ShapeDtypeStruct((db, F_HEADS, LANES), F32)],
        compiler_params=_params("parallel"),
        name="fox_sample",
    )(page_table, q, kn, lfn, msuf, *([cache_kt] * n_pages), *([cache_vt] * n_pages), *([cache_lft] * n_pages))
    past = jnp.swapaxes(ot[:, :, :F_HEADS], 1, 2)
    return (past + stats[:, :, 1:2] * vn) / stats[:, :, 0:1]


def _rms(x, g):
    return x * lax.rsqrt(jnp.mean(x * x, axis=-1, keepdims=True) + EPS) * g


def _head_rms(o, g, heads):
    shp = o.shape
    o = o.reshape(shp[:-1] + (heads, shp[-1] // heads))
    y = o * lax.rsqrt(jnp.mean(o * o, axis=-1, keepdims=True) + EPS)
    return y.reshape(shp) * g


def _prep_layer(l, prm):
    w_in = prm['w_in'][l]
    seg = lambda i: w_in[:, _OFF[i]:_OFF[i + 1]]
    main = jnp.concatenate([seg(0), seg(1), seg(2), seg(5), seg(6), seg(7), seg(9), seg(10), seg(11), seg(12),
                            seg(13), seg(14), seg(15)], axis=1).astype(BF16)
    small = jnp.concatenate([seg(3), seg(4), seg(8), jnp.zeros((D_MODEL, LANES - 16), F32)], axis=1).astype(BF16)
    gate_bias = jnp.concatenate([prm['m_gate_b'][l], prm['f_gate_b'][l], jnp.zeros((LANES - 16,), F32)])[None, :]
    pl_sm = jax.nn.softmax(prm['h_lb_logits'].astype(F32), axis=0)
    cum = jnp.cumsum(pl_sm, axis=0)
    lb = (cum[l] - cum[0])[None, :]
    return dict(
        w_main=main, w_small=small, gate_bias=gate_bias, lb=lb,
        w_ada=prm['w_ada'][l].astype(BF16), b_ada=prm['b_ada'][l], nw=prm['norm_w'][l],
        cw=prm['m_conv_w'][l], cb=prm['m_conv_b'][l][None, :],
        wq=prm['w_mq'][l].astype(BF16), wk=prm['w_mk'][l].astype(BF16),
        hnorm_m=prm['m_hnorm'][l][None, :], skip=prm['m_skip'][l][None, :],
        hnorm_h=prm['h_hnorm'][l][None, :],
        w_pa=prm['w_pa'][l].astype(BF16), w_pb=prm['w_pb'][l].astype(BF16), w_pc=prm['w_pc'][l].astype(BF16),
        w_out=prm['w_out'][l].astype(BF16),
        w_gu=jnp.concatenate([prm['w_ffn_g'][l], prm['w_ffn_u'][l]], axis=1).astype(BF16),
        fcw=prm['ffn_conv_w'][l], fcb=prm['ffn_conv_b'][l], w_d=prm['w_ffn_d'][l].astype(BF16))


def _prompt_layer(x, ada, w):
    b, l, d = x.shape
    m = b * l
    tm = 1024
    sh1, sc1, g1, sh2, sc2, g2 = jnp.split(ada[:, None, :], 6, axis=-1)
    x2 = x.reshape(m, d)
    mul1 = w['nw'][0] * (1.0 + sc1)
    z = _pmm(x2, mul1, sh1, w['w_main'], tm=tm, tn=1024, tiles_per_group=l // tm).reshape(b, l, N_MAIN)
    zs = _pmm(x2, mul1, sh1, w['w_small'], tm=tm, tn=LANES, tiles_per_group=l // tm).reshape(b, l, LANES)

    gcol, cum = _gates(zs, w['gate_bias'], tc=512)
    grow = jnp.swapaxes(gcol[:, :, 0:8], 1, 2)
    fkt = jnp.swapaxes(cum[:, :, LANE_FF:LANE_FF + 8], 1, 2)

    hm, c1, n1, m1 = _mlstm_prompt(z, gcol, grow, w['cw'], w['cb'], w['wq'], w['wk'], w['hnorm_m'], w['skip'], c=128)
    att = _fox_prompt(z, cum, fkt, t=512)
    oh, s1 = _hgrn_prompt(z, w['lb'], w['hnorm_h'])

    ya = _mm(hm.reshape(m, -1), w['w_pa'], tm=tm, tn=1024)
    yb = _mm(att.reshape(m, -1), w['w_pb'], tm=tm, tn=1024)
    yc = _mm(oh.reshape(m, -1), w['w_pc'], tm=tm, tn=1024)
    z2 = z.reshape(m, N_MAIN)
    merged = (jax.nn.sigmoid(z2[:, COL_GA:COL_GA + d]) * ya + jax.nn.sigmoid(z2[:, COL_GB:COL_GB + d]) * yb
              + jax.nn.sigmoid(z2[:, COL_GC:COL_GC + d]) * yc)
    mix = _mm(merged, w['w_out'], tm=tm, tn=1024).reshape(b, l, d)
    x = x + (1.0 + g1) * _rms(mix, w['nw'][1])

    mul2 = w['nw'][2] * (1.0 + sc2)
    gu = _pmm(x.reshape(m, d), mul2, sh2, w['w_gu'], tm=tm, tn=512, tiles_per_group=l // tm).reshape(b, l, 2 * D_FF)
    g, u = gu[..., :D_FF], gu[..., D_FF:]
    gp = jnp.pad(g, ((0, 0), (FFN_CONV - 1, 0), (0, 0)))
    fcw = w['fcw']
    gc = w['fcb'] + fcw[0] * gp[:, 0:l] + fcw[1] * gp[:, 1:l + 1] + fcw[2] * gp[:, 2:l + 2]
    ffo = _mm((jax.nn.gelu(gc) * u).reshape(m, D_FF), w['w_d'], tm=512, tn=1024).reshape(b, l, d)
    x = x + (1.0 + g2) * _rms(ffo, w['nw'][3])

    k = z[..., BLK_FK * 512:(BLK_FK + 1) * 512].reshape(b, l, F_HEADS, F_HD)
    v = z[..., BLK_FV * 512:(BLK_FV + 1) * 512].reshape(b, l, F_HEADS, F_HD)
    lff = gcol[..., LANE_FF:LANE_FF + F_HEADS]
    mconv = z[:, l - (M_CONV - 1):, 0:BRANCH_WIDTH]
    fconv = g[:, l - (FFN_CONV - 1):, :]
    st = (k, v, lff, c1, n1[:, :M_HEADS, :], m1[:, :M_HEADS, 0], mconv, s1, fconv)
    return x, st


def _sample_layer(l, x, ada, w, page_table, cache_k, cache_v, cache_logf, st_c, st_n, st_m, st_mconv, st_h, st_fconv):
    db, d = x.shape
    sh1, sc1, g1, sh2, sc2, g2 = jnp.split(ada, 6, axis=-1)
    mul1 = (w['nw'][0] * (1.0 + sc1))[None]
    z = _pmm(x, mul1, sh1[None], w['w_main'], tm=db, tn=1024, tiles_per_group=1)
    zs = _pmm(x, mul1, sh1[None], w['w_small'], tm=db, tn=LANES, tiles_per_group=1) + w['gate_bias']
    blk = lambda i: z[:, i * 512:(i + 1) * 512]

    mx = blk(BLK_MX)
    cw = w['cw']
    mcv = w['cb'] + cw[0] * st_mconv[:, 0] + cw[1] * st_mconv[:, 1] + cw[2] * st_mconv[:, 2] + cw[3] * mx
    mconv_new = jnp.concatenate([st_mconv[:, 1:], mx[:, None, :]], axis=1)
    ca = jax.nn.silu(mcv)
    mq = _headmm(ca, w['wq']).reshape(db, M_HEADS, M_DK) * (M_DK ** -0.5)
    mk = _headmm(ca, w['wk']).reshape(db, M_HEADS, M_DK)
    ig = zs[:, LANE_MI:LANE_MI + M_HEADS]
    lfm = jax.nn.log_sigmoid(zs[:, LANE_MF:LANE_MF + M_HEADS])
    mn = jnp.maximum(ig, lfm + st_m)
    wk_ = jnp.exp(ig - mn)[..., None]
    dec = jnp.exp(lfm + st_m - mn)[..., None]
    g_ = db * M_HEADS
    c_new, onum = _state_step(st_c.reshape(g_, M_DK, M_DK),
                              jnp.broadcast_to(dec, (db, M_HEADS, M_DK)).reshape(g_, M_DK),
                              (wk_ * mk).reshape(g_, M_DK), mq.reshape(g_, M_DK),
                              blk(BLK_MV).reshape(g_, M_DK))
    n_new = dec * st_n + wk_ * mk
    den = jnp.sum(mq * n_new, axis=-1, keepdims=True)
    hm = onum.reshape(db, M_HEADS, M_DK) / jnp.maximum(jnp.abs(den), jnp.exp(-mn)[..., None])
    hm = (_head_rms(hm.reshape(db, -1), w['hnorm_m'], M_HEADS) + w['skip'] * ca) * jax.nn.silu(blk(BLK_MO))
    ya = _mm(hm, w['w_pa'], tm=db, tn=1024)

    fq = blk(BLK_FQ).reshape(db, F_HEADS, F_HD)
    fk = blk(BLK_FK).reshape(db, F_HEADS, F_HD)
    fv = blk(BLK_FV).reshape(db, F_HEADS, F_HD)
    lff = jax.nn.log_sigmoid(zs[:, LANE_FF:LANE_FF + F_HEADS])
    att = _fox_sample(l, page_table, cache_k, cache_v, cache_logf, fq, fk, fv, lff)
    yb = _mm(att.reshape(db, -1), w['w_pb'], tm=db, tn=1024)

    hf = blk(BLK_HF)
    lb = w['lb']
    f = lb + (1.0 - lb) * jax.nn.sigmoid(hf)
    hk = (1.0 - lb) * jax.nn.sigmoid(-hf)
    hq = jax.nn.silu(blk(BLK_HQ))
    g2_ = db * H_HEADS
    s_new, oh = _state_step(st_h.reshape(g2_, H_DK, H_DK), f.reshape(g2_, H_DK), hk.reshape(g2_, H_DK),
                            hq.reshape(g2_, H_DK), blk(BLK_HI).reshape(g2_, H_DK))
    oh = _head_rms(oh.reshape(db, -1), w['hnorm_h'], H_HEADS) * jax.nn.silu(blk(BLK_HG))
    yc = _mm(oh, w['w_pc'], tm=db, tn=1024)

    merged = (jax.nn.sigmoid(z[:, COL_GA:COL_GA + d]) * ya + jax.nn.sigmoid(z[:, COL_GB:COL_GB + d]) * yb
              + jax.nn.sigmoid(z[:, COL_GC:COL_GC + d]) * yc)
    mix = _mm(merged, w['w_out'], tm=db, tn=1024)
    x = x + (1.0 + g1) * _rms(mix, w['nw'][1])

    mul2 = (w['nw'][2] * (1.0 + sc2))[None]
    gu = _pmm(x, mul2, sh2[None], w['w_gu'], tm=db, tn=512, tiles_per_group=1)
    g, u = gu[:, :D_FF], gu[:, D_FF:]
    fcw = w['fcw']
    gc = w['fcb'] + fcw[0] * st_fconv[:, 0] + fcw[1] * st_fconv[:, 1] + fcw[2] * g
    fconv_new = jnp.concatenate([st_fconv[:, 1:], g[:, None, :]], axis=1)
    ffo = _mm(jax.nn.gelu(gc) * u, w['w_d'], tm=db, tn=1024)
    x = x + (1.0 + g2) * _rms(ffo, w['nw'][3])

    st = (fk[:, None], fv[:, None], lff[:, None], c_new.reshape(db, M_HEADS, M_DK, M_DK), n_new, mn,
          mconv_new, s_new.reshape(db, H_HEADS, H_DK, H_DK), fconv_new)
    return x, st


def kernel(x_prompt, x_sample, c_prompt, c_sample, cache_k, cache_v, cache_logf, page_table, state_mlstm_c, state_mlstm_n, state_mlstm_m, state_mlstm_conv, state_hgrn, state_ffn_conv, norm_w, w_ada, b_ada, w_in, m_conv_w, m_conv_b, w_mq, w_mk, m_gate_b, m_skip, m_hnorm, f_gate_b, h_lb_logits, h_hnorm, w_pa, w_pb, w_pc, w_out, w_ffn_g, w_ffn_u, ffn_conv_w, ffn_conv_b, w_ffn_d):
    prm = dict(norm_w=norm_w, w_ada=w_ada, b_ada=b_ada, w_in=w_in, m_conv_w=m_conv_w, m_conv_b=m_conv_b,
               w_mq=w_mq, w_mk=w_mk, m_gate_b=m_gate_b, m_skip=m_skip, m_hnorm=m_hnorm, f_gate_b=f_gate_b,
               h_lb_logits=h_lb_logits, h_hnorm=h_hnorm, w_pa=w_pa, w_pb=w_pb, w_pc=w_pc, w_out=w_out,
               w_ffn_g=w_ffn_g, w_ffn_u=w_ffn_u, ffn_conv_w=ffn_conv_w, ffn_conv_b=ffn_conv_b, w_ffn_d=w_ffn_d)
    nb = x_prompt.shape[0]
    db = x_sample.shape[0]
    pad = (-(nb + db)) % 16
    c_all = jnp.concatenate([c_sample, c_prompt, jnp.zeros((pad, D_MODEL), F32)], axis=0)
    sc_all = jax.nn.silu(c_all)

    cache_k = jnp.transpose(cache_k, (0, 1, 3, 4, 2))
    cache_v = jnp.transpose(cache_v, (0, 1, 3, 4, 2))
    cache_logf = jnp.transpose(cache_logf, (0, 1, 3, 2))

    xp = x_prompt
    xs = x_sample[:, 0, :]
    pst = [[] for _ in range(9)]
    sst = [[] for _ in range(9)]
    for l in range(DEPTH):
        w = _prep_layer(l, prm)
        ada = _mm(sc_all, w['w_ada'], tm=sc_all.shape[0], tn=1024) + w['b_ada']
        xp, st = _prompt_layer(xp, ada[db:db + nb], w)
        for j in range(9):
            pst[j].append(st[j])
        xs, st = _sample_layer(l, xs, ada[:db], w, page_table, cache_k, cache_v, cache_logf,
                               state_mlstm_c[l], state_mlstm_n[l], state_mlstm_m[l], state_mlstm_conv[l],
                               state_hgrn[l], state_ffn_conv[l])
        for j in range(9):
            sst[j].append(st[j])
    p_out = [jnp.stack(s, axis=0) for s in pst]
    s_out = [jnp.stack(s, axis=0) for s in sst]
    return (xp, xs[:, None, :], *p_out, *s_out)
```

```python
import functools
import math

import numpy as np
import jax
import jax.numpy as jnp
from jax import lax
from jax.experimental import pallas as pl
from jax.experimental.pallas import tpu as pltpu

F32 = jnp.float32
BF16 = jnp.bfloat16
HIGHEST = lax.Precision.HIGHEST

D_MODEL = 1024
DEPTH = 4
PAGE_SIZE = 128
BRANCH_WIDTH = D_MODEL // 2
M_HEADS = 4
M_DK = BRANCH_WIDTH // M_HEADS
M_CONV = 4
F_HEADS = 8
F_HD = BRANCH_WIDTH // F_HEADS
H_HEADS = 4
H_DK = BRANCH_WIDTH // H_HEADS
D_FF = ((8 * D_MODEL // 3 + 127) // 128) * 128
FFN_CONV = 3
EPS = 1e-6
NEG = -1e30

LANES = 128
VMEM_LIMIT = 56 * 1024 * 1024

_SIZES = (512, 512, 512, 4, 4, 512, 512, 512, 8, 512, 512, 512, 512, 1024, 1024, 1024)
_OFF = tuple(int(v) for v in np.concatenate([[0], np.cumsum(_SIZES)]))
BLK_MX, BLK_MV, BLK_MO, BLK_FQ, BLK_FK, BLK_FV, BLK_HQ, BLK_HF, BLK_HI, BLK_HG = range(10)
COL_GA, COL_GB, COL_GC = 5120, 6144, 7168
N_MAIN = 8192
LANE_MI, LANE_MF, LANE_FF = 0, 4, 8


def _nt(a, b):
    return lax.dot_general(a, b, (((1,), (1,)), ((), ())), preferred_element_type=F32)


def _dot(a, b):
    return jnp.dot(a, b, preferred_element_type=F32)


def _log_sigmoid(x):
    return jnp.minimum(x, 0.0) - jnp.log1p(jnp.exp(-jnp.abs(x)))


def _sigmoid(x):
    return 1.0 / (1.0 + jnp.exp(-x))


def _params(*sem):
    return pltpu.CompilerParams(dimension_semantics=sem, vmem_limit_bytes=VMEM_LIMIT)


def _mm_kernel(x_ref, w_ref, o_ref, xs):
    @pl.when(pl.program_id(1) == 0)
    def _():
        xs[...] = x_ref[...].astype(BF16)

    o_ref[...] = _dot(xs[...], w_ref[...])


def _mm(x, w, *, tm, tn):
    m, k = x.shape
    n = w.shape[1]
    return pl.pallas_call(
        _mm_kernel,
        grid=(m // tm, n // tn),
        in_specs=[pl.BlockSpec((tm, k), lambda i, j: (i, 0)),
                  pl.BlockSpec((k, tn), lambda i, j: (0, j))],
        out_specs=pl.BlockSpec((tm, tn), lambda i, j: (i, j)),
        out_shape=jax.ShapeDtypeStruct((m, n), F32),
        scratch_shapes=[pltpu.VMEM((tm, k), BF16)],
        compiler_params=_params("parallel", "arbitrary"),
        name="mm",
    )(x, w)


def _pmm_kernel(x_ref, mul_ref, add_ref, w_ref, o_ref, hs):
    @pl.when(pl.program_id(1) == 0)
    def _():
        x = x_ref[...]
        y = x * lax.rsqrt(jnp.mean(x * x, axis=-1, keepdims=True) + EPS)
        hs[...] = (y * mul_ref[0] + add_ref[0]).astype(BF16)

    o_ref[...] = _dot(hs[...], w_ref[...])


def _pmm(x, mul, add, w, *, tm, tn, tiles_per_group):
    m, k = x.shape
    n = w.shape[1]
    r = mul.shape[1]
    return pl.pallas_call(
        _pmm_kernel,
        grid=(m // tm, n // tn),
        in_specs=[pl.BlockSpec((tm, k), lambda i, j: (i, 0)),
                  pl.BlockSpec((1, r, k), lambda i, j: (i // tiles_per_group, 0, 0)),
                  pl.BlockSpec((1, r, k), lambda i, j: (i // tiles_per_group, 0, 0)),
                  pl.BlockSpec((k, tn), lambda i, j: (0, j))],
        out_specs=pl.BlockSpec((tm, tn), lambda i, j: (i, j)),
        out_shape=jax.ShapeDtypeStruct((m, n), F32),
        scratch_shapes=[pltpu.VMEM((tm, k), BF16)],
        compiler_params=_params("parallel", "arbitrary"),
        name="pmm",
    )(x, mul, add, w)


def _headmm_kernel(x_ref, w_ref, o_ref):
    o_ref[...] = _dot(x_ref[...].astype(BF16), w_ref[0])


def _headmm(x, w):
    m = x.shape[0]
    h, d, _ = w.shape
    return pl.pallas_call(
        _headmm_kernel,
        grid=(h,),
        in_specs=[pl.BlockSpec((m, d), lambda i: (0, i)),
                  pl.BlockSpec((1, d, d), lambda i: (i, 0, 0))],
        out_specs=pl.BlockSpec((m, d), lambda i: (0, i)),
        out_shape=jax.ShapeDtypeStruct((m, h * d), F32),
        compiler_params=_params("parallel"),
        name="headmm",
    )(x, w)


def _merge_kernel(hm_ref, att_ref, oh_ref, ga_ref, gb_ref, gc_ref, x_ref, wpa_ref, wpb_ref, wpc_ref, wout_ref,
                  gate_ref, nw_ref, o_ref):
    ya = _dot(hm_ref[0].astype(BF16), wpa_ref[...])
    yb = _dot(att_ref[0].astype(BF16), wpb_ref[...])
    yc = _dot(oh_ref[0].astype(BF16), wpc_ref[...])
    merged = _sigmoid(ga_ref[0]) * ya + _sigmoid(gb_ref[0]) * yb + _sigmoid(gc_ref[0]) * yc
    mix = _dot(merged.astype(BF16), wout_ref[...])
    y = mix * lax.rsqrt(jnp.mean(mix * mix, axis=-1, keepdims=True) + EPS) * nw_ref[...]
    o_ref[0] = x_ref[0] + gate_ref[0] * y


def _merge_out(hm, att, oh, z, x, wpa, wpb, wpc, wout, gate, nw, *, tm):
    b, l, d = x.shape
    w = BRANCH_WIDTH
    br = pl.BlockSpec((1, tm, w), lambda i, j: (i, j, 0))
    gate_col = lambda c: pl.BlockSpec((1, tm, d), lambda i, j: (i, j, c // d))
    res = lambda shape: pl.BlockSpec(shape, lambda i, j: (0,) * len(shape))
    return pl.pallas_call(
        _merge_kernel,
        grid=(b, l // tm),
        in_specs=[br, br, br, gate_col(COL_GA), gate_col(COL_GB), gate_col(COL_GC),
                  pl.BlockSpec((1, tm, d), lambda i, j: (i, j, 0)),
                  res((w, d)), res((w, d)), res((w, d)), res((d, d)),
                  pl.BlockSpec((1, 1, d), lambda i, j: (i, 0, 0)), res((1, d))],
        out_specs=pl.BlockSpec((1, tm, d), lambda i, j: (i, j, 0)),
        out_shape=jax.ShapeDtypeStruct((b, l, d), F32),
        compiler_params=_params("parallel", "parallel"),
        name="merge_out",
    )(hm, att, oh, z, z, z, x, wpa, wpb, wpc, wout, gate, nw)


def _gelu_tanh(x):
    return 0.5 * x * (1.0 + jnp.tanh(math.sqrt(2.0 / math.pi) * (x + 0.044715 * (x * x * x))))


def _ffn_down_kernel(g_ref, u_ref, x_ref, cw_ref, cb_ref, wd_ref, gate_ref, nw_ref, o_ref, gbuf, *, tm):
    @pl.when(pl.program_id(1) == 0)
    def _():
        gbuf[0:8, :] = jnp.zeros((8, D_FF), F32)

    g = g_ref[0]
    gbuf[8:8 + tm, :] = g
    cw = cw_ref[...]
    gc = cb_ref[...] + cw[2:3, :] * g + cw[1:2, :] * gbuf[7:7 + tm, :] + cw[0:1, :] * gbuf[6:6 + tm, :]
    gbuf[0:8, :] = g[tm - 8:tm, :]
    a = (_gelu_tanh(gc) * u_ref[0]).astype(BF16)
    ffo = _dot(a, wd_ref[...])
    y = ffo * lax.rsqrt(jnp.mean(ffo * ffo, axis=-1, keepdims=True) + EPS) * nw_ref[...]
    o_ref[0] = x_ref[0] + gate_ref[0] * y


def _ffn_down(gu, x, cw, cb, wd, gate, nw, *, tm):
    b, l, d = x.shape
    return pl.pallas_call(
        functools.partial(_ffn_down_kernel, tm=tm),
        grid=(b, l // tm),
        in_specs=[pl.BlockSpec((1, tm, D_FF), lambda i, j: (i, j, 0)),
                  pl.BlockSpec((1, tm, D_FF), lambda i, j: (i, j, 1)),
                  pl.BlockSpec((1, tm, d), lambda i, j: (i, j, 0)),
                  pl.BlockSpec((FFN_CONV, D_FF), lambda i, j: (0, 0)),
                  pl.BlockSpec((1, D_FF), lambda i, j: (0, 0)),
                  pl.BlockSpec((D_FF, d), lambda i, j: (0, 0)),
                  pl.BlockSpec((1, 1, d), lambda i, j: (i, 0, 0)),
                  pl.BlockSpec((1, d), lambda i, j: (0, 0))],
        out_specs=pl.BlockSpec((1, tm, d), lambda i, j: (i, j, 0)),
        out_shape=jax.ShapeDtypeStruct((b, l, d), F32),
        scratch_shapes=[pltpu.VMEM((tm + 8, D_FF), F32)],
        compiler_params=_params("parallel", "arbitrary"),
        name="ffn_down",
    )(gu, gu, x, cw, cb, wd, gate, nw)


def _gates_kernel(zs_ref, bias_ref, mcum_ref, g_ref, cum_ref, carry):
    @pl.when(pl.program_id(1) == 0)
    def _():
        carry[...] = jnp.zeros_like(carry)

    z = zs_ref[0] + bias_ref[...]
    ls = _log_sigmoid(z)
    lane = lax.broadcasted_iota(jnp.int32, z.shape, 1)
    g_ref[0] = jnp.where(lane < LANE_MF, z, ls)
    cum = jnp.dot(mcum_ref[...], ls, precision=HIGHEST, preferred_element_type=F32) + carry[0:1, :]
    cum_ref[0] = cum
    tc = z.shape[0]
    carry[...] = jnp.broadcast_to(cum[tc - 1:tc, :], carry.shape)


def _gates(zs, bias, *, tc):
    b, l, _ = zs.shape
    mcum = jnp.asarray(np.tril(np.ones((tc, tc), np.float32)))
    return pl.pallas_call(
        _gates_kernel,
        grid=(b, l // tc),
        in_specs=[pl.BlockSpec((1, tc, LANES), lambda i, c: (i, c, 0)),
                  pl.BlockSpec((1, LANES), lambda i, c: (0, 0)),
                  pl.BlockSpec((tc, tc), lambda i, c: (0, 0))],
        out_specs=[pl.BlockSpec((1, tc, LANES), lambda i, c: (i, c, 0)),
                   pl.BlockSpec((1, tc, LANES), lambda i, c: (i, c, 0))],
        out_shape=[jax.ShapeDtypeStruct((b, l, LANES), F32),
                   jax.ShapeDtypeStruct((b, l, LANES), F32)],
        scratch_shapes=[pltpu.VMEM((8, LANES), F32)],
        compiler_params=_params("parallel", "arbitrary"),
        name="gates",
    )(zs, bias, mcum)


def _mlstm_kernel(mx_ref, mv_ref, mo_ref, gc_ref, gr_ref, cw_ref, cb_ref, wq_ref, wk_ref,
                  hn_ref, sk_ref, mcum_ref,
                  out_ref, cfin_ref, nfin_ref, mfin_ref,
                  xbuf, c_sc, n_sc, m_sc, *, c):
    ci = pl.program_id(1)

    @pl.when(ci == 0)
    def _():
        xbuf[0:8, :] = jnp.zeros((8, BRANCH_WIDTH), F32)
        c_sc[...] = jnp.zeros_like(c_sc)
        n_sc[...] = jnp.zeros_like(n_sc)
        m_sc[...] = jnp.zeros_like(m_sc)

    x = mx_ref[0]
    xbuf[8:8 + c, :] = x
    cw = cw_ref[...]
    mcv = (cb_ref[...] + cw[3:4, :] * x + cw[2:3, :] * xbuf[7:7 + c, :]
           + cw[1:2, :] * xbuf[6:6 + c, :] + cw[0:1, :] * xbuf[5:5 + c, :])
    xbuf[0:8, :] = x[c - 8:c, :]
    ca = mcv * _sigmoid(mcv)

    gcol = gc_ref[0]
    grow = gr_ref[0]
    mcum = mcum_ref[...]
    fcol = jnp.dot(mcum, gcol, precision=HIGHEST, preferred_element_type=F32)
    frow = lax.dot_general(grow, mcum, (((1,), (1,)), ((), ())), precision=HIGHEST,
                           preferred_element_type=F32)
    tri = (lax.broadcasted_iota(jnp.int32, (c, c), 0) >= lax.broadcasted_iota(jnp.int32, (c, c), 1))
    scale = M_DK ** -0.5
    mv = mv_ref[0]
    mo = mo_ref[0]

    for h in range(M_HEADS):
        sl = slice(h * M_DK, (h + 1) * M_DK)
        cah = ca[:, sl]
        cb16 = cah.astype(BF16)
        q = _dot(cb16, wq_ref[h]) * scale
        k = _dot(cb16, wk_ref[h])
        v = mv[:, sl]
        q16 = q.astype(BF16)
        v16 = v.astype(BF16)
        f_c = fcol[:, LANE_MF + h:LANE_MF + h + 1]
        ig_c = gcol[:, h:h + 1]
        f_r = frow[LANE_MF + h:LANE_MF + h + 1, :]
        ig_r = grow[h:h + 1, :]
        m_prev = m_sc[h:h + 1, 0:1]

        d = jnp.where(tri, f_c - f_r + ig_r, NEG)
        inter = f_c + m_prev
        mt = jnp.maximum(jnp.max(d, axis=-1, keepdims=True), inter)
        a = _nt(q16, k.astype(BF16)) * jnp.exp(d - mt)
        ei = jnp.exp(inter - mt)
        num = _dot(a.astype(BF16), v16) + ei * _dot(q16, c_sc[h].astype(BF16))
        den = jnp.sum(a, axis=-1, keepdims=True) + ei * jnp.sum(q * n_sc[h:h + 1, :], axis=-1, keepdims=True)
        hh = num / jnp.maximum(jnp.abs(den), jnp.exp(-mt))

        fl = f_c[c - 1:c, :]
        wl_c = fl - f_c + ig_c
        wl_r = fl - f_r + ig_r
        mn = jnp.maximum(fl + m_prev, jnp.max(wl_r, axis=-1, keepdims=True))
        dec = jnp.exp(fl + m_prev - mn)
        kw = k * jnp.exp(wl_c - mn)
        c_sc[h] = dec * c_sc[h] + _dot(kw.T.astype(BF16), v16)
        n_sc[h:h + 1, :] = dec * n_sc[h:h + 1, :] + jnp.sum(kw, axis=0, keepdims=True)
        m_sc[h:h + 1, :] = jnp.broadcast_to(mn, (1, LANES))

        hn = hh * lax.rsqrt(jnp.mean(hh * hh, axis=-1, keepdims=True) + EPS)
        hn = hn * hn_ref[:, sl] + sk_ref[:, sl] * cah
        moh = mo[:, sl]
        out_ref[0, :, sl] = hn * (moh * _sigmoid(moh))

    @pl.when(ci == pl.num_programs(1) - 1)
    def _():
        cfin_ref[0] = c_sc[...]
        nfin_ref[0] = n_sc[...]
        mfin_ref[0] = m_sc[...]


def _mlstm_prompt(z, gcol, grow, cw, cb, wq, wk, hnorm, skip, *, c):
    b, l, _ = z.shape
    mcum = jnp.asarray(np.tril(np.ones((c, c), np.float32)))
    w = BRANCH_WIDTH
    full = lambda shape: pl.BlockSpec(shape, lambda i, j: (0,) * len(shape))
    return pl.pallas_call(
        functools.partial(_mlstm_kernel, c=c),
        grid=(b, l // c),
        in_specs=[pl.BlockSpec((1, c, w), lambda i, j: (i, j, BLK_MX)),
                  pl.BlockSpec((1, c, w), lambda i, j: (i, j, BLK_MV)),
                  pl.BlockSpec((1, c, w), lambda i, j: (i, j, BLK_MO)),
                  pl.BlockSpec((1, c, LANES), lambda i, j: (i, j, 0)),
                  pl.BlockSpec((1, 8, c), lambda i, j: (i, 0, j)),
                  full((M_CONV, w)), full((1, w)),
                  full((M_HEADS, M_DK, M_DK)), full((M_HEADS, M_DK, M_DK)),
                  full((1, w)), full((1, w)), full((c, c))],
        out_specs=[pl.BlockSpec((1, c, w), lambda i, j: (i, j, 0)),
                   pl.BlockSpec((1, M_HEADS, M_DK, M_DK), lambda i, j: (i, 0, 0, 0)),
                   pl.BlockSpec((1, 8, LANES), lambda i, j: (i, 0, 0)),
                   pl.BlockSpec((1, 8, LANES), lambda i, j: (i, 0, 0))],
        out_shape=[jax.ShapeDtypeStruct((b, l, w), F32),
                   jax.ShapeDtypeStruct((b, M_HEADS, M_DK, M_DK), F32),
                   jax.ShapeDtypeStruct((b, 8, LANES), F32),
                   jax.ShapeDtypeStruct((b, 8, LANES), F32)],
        scratch_shapes=[pltpu.VMEM((c + 8, w), F32),
                        pltpu.VMEM((M_HEADS, M_DK, M_DK), F32),
                        pltpu.VMEM((8, LANES), F32),
                        pltpu.VMEM((8, LANES), F32)],
        compiler_params=_params("parallel", "arbitrary"),
        name="mlstm_prompt",
    )(z, z, z, gcol, grow, cw, cb, wq, wk, hnorm, skip, mcum)


def _fox_step(masked, q_ref, k_ref, v_ref, fq_ref, fk_ref, m_sc, l_sc, acc_sc, tq, tk):
    q = q_ref[0] * (F_HD ** -0.5)
    k = k_ref[0].astype(BF16)
    v = v_ref[0].astype(BF16)
    fq = fq_ref[0]
    fk = fk_ref[0]
    lo = lax.broadcasted_iota(jnp.int32, (tq, LANES), 1) < F_HD
    if masked:
        tri = (lax.broadcasted_iota(jnp.int32, (tq, tk), 0) >= lax.broadcasted_iota(jnp.int32, (tq, tk), 1))
    for pr in range(F_HEADS // 2):
        sl = slice(pr * LANES, (pr + 1) * LANES)
        q2, k2, v2 = q[:, sl], k[:, sl], v[:, sl]
        alphas, pvs = [], []
        for hh in range(2):
            h = 2 * pr + hh
            qm = jnp.where(lo if hh == 0 else jnp.logical_not(lo), q2, 0.0).astype(BF16)
            s = _nt(qm, k2) + fq[:, LANE_FF + h:LANE_FF + h + 1] - fk[h:h + 1, :]
            if masked:
                s = jnp.where(tri, s, NEG)
            m_prev = m_sc[:, h:h + 1]
            m_new = jnp.maximum(m_prev, jnp.max(s, axis=-1, keepdims=True))
            alpha = jnp.exp(m_prev - m_new)
            p = jnp.exp(s - m_new)
            l_sc[:, h:h + 1] = alpha * l_sc[:, h:h + 1] + jnp.sum(p, axis=-1, keepdims=True)
            m_sc[:, h:h + 1] = m_new
            pvs.append(_dot(p.astype(BF16), v2))
            alphas.append(alpha)
        acc_sc[pr] = acc_sc[pr] * jnp.where(lo, alphas[0], alphas[1]) + jnp.where(lo, pvs[0], pvs[1])


def _fox_kernel(qi_ref, kj_ref, q_ref, k_ref, v_ref, fq_ref, fk_ref, o_ref, m_sc, l_sc, acc_sc, *, tq, tk):
    s = pl.program_id(1)
    i = qi_ref[s]
    j = kj_ref[s]

    @pl.when(j == 0)
    def _():
        m_sc[...] = jnp.full(m_sc.shape, NEG, F32)
        l_sc[...] = jnp.zeros_like(l_sc)
        acc_sc[...] = jnp.zeros_like(acc_sc)

    @pl.when(j < i)
    def _():
        _fox_step(False, q_ref, k_ref, v_ref, fq_ref, fk_ref, m_sc, l_sc, acc_sc, tq, tk)

    @pl.when(j == i)
    def _():
        _fox_step(True, q_ref, k_ref, v_ref, fq_ref, fk_ref, m_sc, l_sc, acc_sc, tq, tk)
        lo = lax.broadcasted_iota(jnp.int32, (tq, LANES), 1) < F_HD
        for pr in range(F_HEADS // 2):
            inv = jnp.where(lo, 1.0 / l_sc[:, 2 * pr:2 * pr + 1], 1.0 / l_sc[:, 2 * pr + 1:2 * pr + 2])
            o_ref[0, :, pr * LANES:(pr + 1) * LANES] = acc_sc[pr] * inv


def _fox_prompt(z, cum, fkt, *, t):
    b, l, _ = z.shape
    nb = l // t
    qi = np.array([i for i in range(nb) for _ in range(i + 1)], np.int32)
    kj = np.array([j for i in range(nb) for j in range(i + 1)], np.int32)
    w = BRANCH_WIDTH
    grid_spec = pltpu.PrefetchScalarGridSpec(
        num_scalar_prefetch=2,
        grid=(b, len(qi)),
        in_specs=[pl.BlockSpec((1, t, w), lambda i, s, qi, kj: (i, qi[s], BLK_FQ)),
                  pl.BlockSpec((1, t, w), lambda i, s, qi, kj: (i, kj[s], BLK_FK)),
                  pl.BlockSpec((1, t, w), lambda i, s, qi, kj: (i, kj[s], BLK_FV)),
                  pl.BlockSpec((1, t, LANES), lambda i, s, qi, kj: (i, qi[s], 0)),
                  pl.BlockSpec((1, 8, t), lambda i, s, qi, kj: (i, 0, kj[s]))],
        out_specs=pl.BlockSpec((1, t, w), lambda i, s, qi, kj: (i, qi[s], 0)),
        scratch_shapes=[pltpu.VMEM((t, LANES), F32), pltpu.VMEM((t, LANES), F32),
                        pltpu.VMEM((F_HEADS // 2, t, LANES), F32)])
    return pl.pallas_call(
        functools.partial(_fox_kernel, tq=t, tk=t),
        grid_spec=grid_spec,
        out_shape=jax.ShapeDtypeStruct((b, l, w), F32),
        compiler_params=_params("parallel", "arbitrary"),
        name="fox_prompt",
    )(jnp.asarray(qi), jnp.asarray(kj), z, z, z, cum, fkt)


HG_C = 128
HG_LEVELS = 7


def _hgrn_mats(c):
    mats = [np.tril(np.ones((c, c), np.float32))]
    for lev in range(HG_LEVELS):
        s = (c // 2) >> lev
        m = np.zeros((c, c), np.float32)
        for t in range(c):
            ref = (t // (2 * s)) * 2 * s + s - 1
            if t % (2 * s) >= s:
                m[t, ref + 1:t + 1] = 1.0
            else:
                m[t, t + 1:ref + 1] = 1.0
        mats.append(m)
    mats.append(np.triu(np.ones((c, c), np.float32), 1))
    return np.concatenate(mats, axis=0)


def _hgrn_kernel(hq_ref, hf_ref, hi_ref, hg_ref, lb_ref, hn_ref, mall_ref, o_ref, sfin_ref, st_sc):
    c = HG_C
    ci = pl.program_id(2)

    @pl.when(ci == 0)
    def _():
        st_sc[...] = jnp.zeros_like(st_sc)

    hf = hf_ref[0]
    lb = lb_ref[...]
    la = jnp.log(lb + (1.0 - lb) * _sigmoid(hf))
    k = (1.0 - lb) * _sigmoid(-hf)
    hq = hq_ref[0]
    q = hq * _sigmoid(hq)
    v16 = hi_ref[0].astype(BF16)

    args = jnp.dot(mall_ref[...], la, precision=HIGHEST, preferred_element_type=F32)
    b = args[0:c]
    ti = lax.broadcasted_iota(jnp.int32, (c, c), 0)
    si = lax.broadcasted_iota(jnp.int32, (c, c), 1)
    rowi = lax.broadcasted_iota(jnp.int32, (c, H_DK), 0)
    a = jnp.where(ti == si, _nt(q.astype(BF16), k.astype(BF16)), 0.0)
    for lev in range(HG_LEVELS):
        sh = int(math.log2(c // 2)) - lev
        upper = ((rowi >> sh) & 1) == 1
        x16 = (jnp.where(upper, q, k) * jnp.exp(args[(lev + 1) * c:(lev + 2) * c])).astype(BF16)
        p = _nt(x16, x16)
        mask = ((ti >> (sh + 1)) == (si >> (sh + 1))) & (((ti >> sh) & 1) == 1) & (((si >> sh) & 1) == 0)
        a = a + jnp.where(mask, p, 0.0)

    st = st_sc[...]
    o = _dot(a.astype(BF16), v16) + _nt((q * jnp.exp(b)).astype(BF16), st.astype(BF16))
    kt = k * jnp.exp(args[(HG_LEVELS + 1) * c:(HG_LEVELS + 2) * c])
    st_new = st * jnp.exp(b[c - 1:c, :]) + _dot(hi_ref[0].T.astype(BF16), kt.astype(BF16))
    st_sc[...] = st_new

    hg = hg_ref[0]
    on = o * lax.rsqrt(jnp.mean(o * o, axis=-1, keepdims=True) + EPS)
    o_ref[0] = on * hn_ref[...] * (hg * _sigmoid(hg))

    @pl.when(ci == pl.num_programs(2) - 1)
    def _():
        sfin_ref[0, 0] = st_new.T


def _hgrn_prompt(z, lb, hnorm):
    b, l, _ = z.shape
    c = HG_C
    mall = jnp.asarray(_hgrn_mats(c))
    blk = lambda base: pl.BlockSpec((1, c, H_DK), lambda i, h, j: (i, j, base * 4 + h))
    return pl.pallas_call(
        _hgrn_kernel,
        grid=(b, H_HEADS, l // c),
        in_specs=[blk(BLK_HQ), blk(BLK_HF), blk(BLK_HI), blk(BLK_HG),
                  pl.BlockSpec((1, H_DK), lambda i, h, j: (0, h)),
                  pl.BlockSpec((1, H_DK), lambda i, h, j: (0, h)),
                  pl.BlockSpec(mall.shape, lambda i, h, j: (0, 0))],
        out_specs=[pl.BlockSpec((1, c, H_DK), lambda i, h, j: (i, j, h)),
                   pl.BlockSpec((1, 1, H_DK, H_DK), lambda i, h, j: (i, h, 0, 0))],
        out_shape=[jax.ShapeDtypeStruct((b, l, BRANCH_WIDTH), F32),
                   jax.ShapeDtypeStruct((b, H_HEADS, H_DK, H_DK), F32)],
        scratch_shapes=[pltpu.VMEM((H_DK, H_DK), F32)],
        compiler_params=_params("parallel", "parallel", "arbitrary"),
        name="hgrn_prompt",
    )(z, z, z, z, lb, hnorm, mall)


def _state_step_kernel(st_ref, at_ref, bt_ref, qt_ref, v_ref, so_ref, o_ref, *, gb):
    at = at_ref[0]
    bt = bt_ref[0]
    qt = qt_ref[0]
    v = v_ref[...]
    for j in range(gb):
        s_new = st_ref[j] * at[:, j:j + 1] + bt[:, j:j + 1] * v[j:j + 1, :]
        so_ref[j] = s_new
        o_ref[j:j + 1, :] = jnp.sum(s_new * qt[:, j:j + 1], axis=0, keepdims=True)


def _state_step(state, a, b, q, v, *, gb=32):
    g, dk, dv = state.shape
    tr = lambda x: x.reshape(g // gb, gb, dk).transpose(0, 2, 1)
    return pl.pallas_call(
        functools.partial(_state_step_kernel, gb=gb),
        grid=(g // gb,),
        in_specs=[pl.BlockSpec((gb, dk, dv), lambda i: (i, 0, 0)),
                  pl.BlockSpec((1, dk, gb), lambda i: (i, 0, 0)),
                  pl.BlockSpec((1, dk, gb), lambda i: (i, 0, 0)),
                  pl.BlockSpec((1, dk, gb), lambda i: (i, 0, 0)),
                  pl.BlockSpec((gb, dv), lambda i: (i, 0))],
        out_specs=[pl.BlockSpec((gb, dk, dv), lambda i: (i, 0, 0)),
                   pl.BlockSpec((gb, dv), lambda i: (i, 0))],
        out_shape=[jax.ShapeDtypeStruct((g, dk, dv), F32),
                   jax.ShapeDtypeStruct((g, dv), F32)],
        compiler_params=_params("parallel"),
        name="state_step",
    )(state, tr(a), tr(b), tr(q), v)


def _fox_sample_kernel(pt_ref, q_ref, kn_ref, lfn_ref, msuf_ref, *rest, n_pages):
    k_refs = rest[:n_pages]
    v_refs = rest[n_pages:2 * n_pages]
    lf_refs = rest[2 * n_pages:3 * n_pages]
    ot_ref, st_ref = rest[3 * n_pages:]

    q = q_ref[0] * (F_HD ** -0.5)
    eye = (lax.broadcasted_iota(jnp.int32, (F_HD, F_HD), 0) == lax.broadcasted_iota(jnp.int32, (F_HD, F_HD), 1))
    ones = jnp.ones((F_HD, PAGE_SIZE), F32)
    qb = [jnp.dot(jnp.where(eye, q[h:h + 1, :], 0.0), ones, precision=HIGHEST, preferred_element_type=F32)
          for h in range(F_HEADS)]
    hrow = lax.broadcasted_iota(jnp.int32, (F_HEADS, PAGE_SIZE), 0)
    lfn = lfn_ref[0]

    carry = jnp.zeros((F_HEADS, 1), F32)
    tiles = [None] * n_pages
    for pg in reversed(range(n_pages)):
        lf = lf_refs[pg][0, 0]
        after = jnp.dot(lf, msuf_ref[...], precision=HIGHEST, preferred_element_type=F32) + carry
        carry = carry + jnp.sum(lf, axis=-1, keepdims=True)
        s = after + lfn
        for h in range(F_HEADS):
            row = jnp.sum(k_refs[pg][0, 0, h] * qb[h], axis=0, keepdims=True)
            s = s + jnp.where(hrow == h, row, 0.0)
        tiles[pg] = s

    s_new = jnp.sum(q * kn_ref[0], axis=-1, keepdims=True)
    m = s_new
    for t in tiles:
        m = jnp.maximum(m, jnp.max(t, axis=-1, keepdims=True))
    p_new = jnp.exp(s_new - m)
    l = p_new
    probs = []
    for t in tiles:
        p = jnp.exp(t - m)
        l = l + jnp.sum(p, axis=-1, keepdims=True)
        probs.append(p)

    lane = lax.broadcasted_iota(jnp.int32, (F_HD, LANES), 1)
    ot = jnp.zeros((F_HD, LANES), F32)
    for h in range(F_HEADS):
        acc = jnp.zeros((F_HD, PAGE_SIZE), F32)
        for pg in range(n_pages):
            acc = acc + v_refs[pg][0, 0, h] * probs[pg][h:h + 1, :]
        ot = jnp.where(lane == h, jnp.sum(acc, axis=-1, keepdims=True), ot)
    ot_ref[0] = ot
    slane = lax.broadcasted_iota(jnp.int32, (F_HEADS, LANES), 1)
    st_ref[0] = jnp.where(slane == 0, l, jnp.where(slane == 1, p_new, 0.0))


def _fox_sample(layer, page_table, cache_kt, cache_vt, cache_lft, q, kn, vn, lff):
    db, n_pages = page_table.shape
    msuf = jnp.asarray(np.tril(np.ones((PAGE_SIZE, PAGE_SIZE), np.float32), -1))
    lfn = jnp.broadcast_to(lff[:, :, None], (db, F_HEADS, LANES))

    def kv_spec(pg):
        return pl.BlockSpec((1, 1, F_HEADS, F_HD, PAGE_SIZE), lambda i, pt: (layer, pt[i, pg], 0, 0, 0))

    def lf_spec(pg):
        return pl.BlockSpec((1, 1, F_HEADS, PAGE_SIZE), lambda i, pt: (layer, pt[i, pg], 0, 0))

    tok = pl.BlockSpec((1, F_HEADS, F_HD), lambda i, pt: (i, 0, 0))
    grid_spec = pltpu.PrefetchScalarGridSpec(
        num_scalar_prefetch=1,
        grid=(db,),
        in_specs=[tok, tok,
                  pl.BlockSpec((1, F_HEADS, LANES), lambda i, pt: (i, 0, 0)),
                  pl.BlockSpec((PAGE_SIZE, PAGE_SIZE), lambda i, pt: (0, 0))]
        + [kv_spec(pg) for pg in range(n_pages)] + [kv_spec(pg) for pg in range(n_pages)]
        + [lf_spec(pg) for pg in range(n_pages)],
        out_specs=[pl.BlockSpec((1, F_HD, LANES), lambda i, pt: (i, 0, 0)),
                   pl.BlockSpec((1, F_HEADS, LANES), lambda i, pt: (i, 0, 0))])
    ot, stats = pl.pallas_call(
        functools.partial(_fox_sample_kernel, n_pages=n_pages),
        grid_spec=grid_spec,
        out_shape=[jax.ShapeDtypeStruct((db, F_HD, LANES), F32),
                   jax.ShapeDtypeStruct((db, F_HEADS, LANES), F32)],
        compiler_params=_params("parallel"),
        name="fox_sample",
    )(page_table, q, kn, lfn, msuf, *([cache_kt] * n_pages), *([cache_vt] * n_pages), *([cache_lft] * n_pages))
    past = jnp.swapaxes(ot[:, :, :F_HEADS], 1, 2)
    return (past + stats[:, :, 1:2] * vn) / stats[:, :, 0:1]


def _rms(x, g):
    return x * lax.rsqrt(jnp.mean(x * x, axis=-1, keepdims=True) + EPS) * g


def _head_rms(o, g, heads):
    shp = o.shape
    o = o.reshape(shp[:-1] + (heads, shp[-1] // heads))
    y = o * lax.rsqrt(jnp.mean(o * o, axis=-1, keepdims=True) + EPS)
    return y.reshape(shp) * g


def _prep_layer(l, prm):
    w_in = prm['w_in'][l]
    seg = lambda i: w_in[:, _OFF[i]:_OFF[i + 1]]
    main = jnp.concatenate([seg(0), seg(1), seg(2), seg(5), seg(6), seg(7), seg(9), seg(10), seg(11), seg(12),
                            seg(13), seg(14), seg(15)], axis=1).astype(BF16)
    small = jnp.concatenate([seg(3), seg(4), seg(8), jnp.zeros((D_MODEL, LANES - 16), F32)], axis=1).astype(BF16)
    gate_bias = jnp.concatenate([prm['m_gate_b'][l], prm['f_gate_b'][l], jnp.zeros((LANES - 16,), F32)])[None, :]
    pl_sm = jax.nn.softmax(prm['h_lb_logits'].astype(F32), axis=0)
    cum = jnp.cumsum(pl_sm, axis=0)
    lb = (cum[l] - cum[0])[None, :]
    return dict(
        w_main=main, w_small=small, gate_bias=gate_bias, lb=lb,
        w_ada=prm['w_ada'][l].astype(BF16), b_ada=prm['b_ada'][l], nw=prm['norm_w'][l],
        cw=prm['m_conv_w'][l], cb=prm['m_conv_b'][l][None, :],
        wq=prm['w_mq'][l].astype(BF16), wk=prm['w_mk'][l].astype(BF16),
        hnorm_m=prm['m_hnorm'][l][None, :], skip=prm['m_skip'][l][None, :],
        hnorm_h=prm['h_hnorm'][l][None, :],
        w_pa=prm['w_pa'][l].astype(BF16), w_pb=prm['w_pb'][l].astype(BF16), w_pc=prm['w_pc'][l].astype(BF16),
        w_out=prm['w_out'][l].astype(BF16),
        w_gu=jnp.concatenate([prm['w_ffn_g'][l], prm['w_ffn_u'][l]], axis=1).astype(BF16),
        fcw=prm['ffn_conv_w'][l], fcb=prm['ffn_conv_b'][l], w_d=prm['w_ffn_d'][l].astype(BF16))


def _prompt_layer(x, ada, w):
    b, l, d = x.shape
    m = b * l
    tm = 1024
    sh1, sc1, g1, sh2, sc2, g2 = jnp.split(ada[:, None, :], 6, axis=-1)
    x2 = x.reshape(m, d)
    mul1 = w['nw'][0] * (1.0 + sc1)
    z = _pmm(x2, mul1, sh1, w['w_main'], tm=tm, tn=1024, tiles_per_group=l // tm).reshape(b, l, N_MAIN)
    zs = _pmm(x2, mul1, sh1, w['w_small'], tm=tm, tn=LANES, tiles_per_group=l // tm).reshape(b, l, LANES)

    gcol, cum = _gates(zs, w['gate_bias'], tc=512)
    grow = jnp.swapaxes(gcol[:, :, 0:8], 1, 2)

    hm, c1, n1, m1 = _mlstm_prompt(z, gcol, grow, w['cw'], w['cb'], w['wq'], w['wk'], w['hnorm_m'], w['skip'], c=128)
    fkt = jnp.swapaxes(cum[:, :, LANE_FF:LANE_FF + 8], 1, 2)
    att = _fox_prompt(z, cum, fkt, t=512)
    oh, s1 = _hgrn_prompt(z, w['lb'], w['hnorm_h'])

    x = _merge_out(hm, att, oh, z, x, w['w_pa'], w['w_pb'], w['w_pc'], w['w_out'], 1.0 + g1,
                   w['nw'][1][None, :], tm=256)

    mul2 = w['nw'][2] * (1.0 + sc2)
    gu = _pmm(x.reshape(m, d), mul2, sh2, w['w_gu'], tm=tm, tn=512, tiles_per_group=l // tm).reshape(b, l, 2 * D_FF)
    x = _ffn_down(gu, x, w['fcw'], w['fcb'][None, :], w['w_d'], 1.0 + g2, w['nw'][3][None, :], tm=256)

    k = z[..., BLK_FK * 512:(BLK_FK + 1) * 512].reshape(b, l, F_HEADS, F_HD)
    v = z[..., BLK_FV * 512:(BLK_FV + 1) * 512].reshape(b, l, F_HEADS, F_HD)
    lff = gcol[..., LANE_FF:LANE_FF + F_HEADS]
    mconv = z[:, l - (M_CONV - 1):, 0:BRANCH_WIDTH]
    fconv = gu[:, l - (FFN_CONV - 1):, :D_FF]
    st = (k, v, lff, c1, n1[:, :M_HEADS, :], m1[:, :M_HEADS, 0], mconv, s1, fconv)
    return x, st


def _sample_layer(l, x, ada, w, page_table, cache_k, cache_v, cache_logf, st_c, st_n, st_m, st_mconv, st_h, st_fconv):
    db, d = x.shape
    sh1, sc1, g1, sh2, sc2, g2 = jnp.split(ada, 6, axis=-1)
    mul1 = (w['nw'][0] * (1.0 + sc1))[None]
    z = _pmm(x, mul1, sh1[None], w['w_main'], tm=db, tn=1024, tiles_per_group=1)
    zs = _pmm(x, mul1, sh1[None], w['w_small'], tm=db, tn=LANES, tiles_per_group=1) + w['gate_bias']
    blk = lambda i: z[:, i * 512:(i + 1) * 512]

    mx = blk(BLK_MX)
    cw = w['cw']
    mcv = w['cb'] + cw[0] * st_mconv[:, 0] + cw[1] * st_mconv[:, 1] + cw[2] * st_mconv[:, 2] + cw[3] * mx
    mconv_new = jnp.concatenate([st_mconv[:, 1:], mx[:, None, :]], axis=1)
    ca = jax.nn.silu(mcv)
    mq = _headmm(ca, w['wq']).reshape(db, M_HEADS, M_DK) * (M_DK ** -0.5)
    mk = _headmm(ca, w['wk']).reshape(db, M_HEADS, M_DK)
    ig = zs[:, LANE_MI:LANE_MI + M_HEADS]
    lfm = jax.nn.log_sigmoid(zs[:, LANE_MF:LANE_MF + M_HEADS])
    mn = jnp.maximum(ig, lfm + st_m)
    wk_ = jnp.exp(ig - mn)[..., None]
    dec = jnp.exp(lfm + st_m - mn)[..., None]
    g_ = db * M_HEADS
    c_new, onum = _state_step(st_c.reshape(g_, M_DK, M_DK),
                              jnp.broadcast_to(dec, (db, M_HEADS, M_DK)).reshape(g_, M_DK),
                              (wk_ * mk).reshape(g_, M_DK), mq.reshape(g_, M_DK),
                              blk(BLK_MV).reshape(g_, M_DK))
    n_new = dec * st_n + wk_ * mk
    den = jnp.sum(mq * n_new, axis=-1, keepdims=True)
    hm = onum.reshape(db, M_HEADS, M_DK) / jnp.maximum(jnp.abs(den), jnp.exp(-mn)[..., None])
    hm = (_head_rms(hm.reshape(db, -1), w['hnorm_m'], M_HEADS) + w['skip'] * ca) * jax.nn.silu(blk(BLK_MO))
    ya = _mm(hm, w['w_pa'], tm=db, tn=1024)

    fq = blk(BLK_FQ).reshape(db, F_HEADS, F_HD)
    fk = blk(BLK_FK).reshape(db, F_HEADS, F_HD)
    fv = blk(BLK_FV).reshape(db, F_HEADS, F_HD)
    lff = jax.nn.log_sigmoid(zs[:, LANE_FF:LANE_FF + F_HEADS])
    att = _fox_sample(l, page_table, cache_k, cache_v, cache_logf, fq, fk, fv, lff)
    yb = _mm(att.reshape(db, -1), w['w_pb'], tm=db, tn=1024)

    hf = blk(BLK_HF)
    lb = w['lb']
    f = lb + (1.0 - lb) * jax.nn.sigmoid(hf)
    hk = (1.0 - lb) * jax.nn.sigmoid(-hf)
    hq = jax.nn.silu(blk(BLK_HQ))
    g2_ = db * H_HEADS
    s_new, oh = _state_step(st_h.reshape(g2_, H_DK, H_DK), f.reshape(g2_, H_DK), hk.reshape(g2_, H_DK),
                            hq.reshape(g2_, H_DK), blk(BLK_HI).reshape(g2_, H_DK))
    oh = _head_rms(oh.reshape(db, -1), w['hnorm_h'], H_HEADS) * jax.nn.silu(blk(BLK_HG))
    yc = _mm(oh, w['w_pc'], tm=db, tn=1024)

    merged = (jax.nn.sigmoid(z[:, COL_GA:COL_GA + d]) * ya + jax.nn.sigmoid(z[:, COL_GB:COL_GB + d]) * yb
              + jax.nn.sigmoid(z[:, COL_GC:COL_GC + d]) * yc)
    mix = _mm(merged, w['w_out'], tm=db, tn=1024)
    x = x + (1.0 + g1) * _rms(mix, w['nw'][1])

    mul2 = (w['nw'][2] * (1.0 + sc2))[None]
    gu = _pmm(x, mul2, sh2[None], w['w_gu'], tm=db, tn=512, tiles_per_group=1)
    g, u = gu[:, :D_FF], gu[:, D_FF:]
    fcw = w['fcw']
    gc = w['fcb'] + fcw[0] * st_fconv[:, 0] + fcw[1] * st_fconv[:, 1] + fcw[2] * g
    fconv_new = jnp.concatenate([st_fconv[:, 1:], g[:, None, :]], axis=1)
    ffo = _mm(jax.nn.gelu(gc) * u, w['w_d'], tm=db, tn=1024)
    x = x + (1.0 + g2) * _rms(ffo, w['nw'][3])

    st = (fk[:, None], fv[:, None], lff[:, None], c_new.reshape(db, M_HEADS, M_DK, M_DK), n_new, mn,
          mconv_new, s_new.reshape(db, H_HEADS, H_DK, H_DK), fconv_new)
    return x, st


def kernel(x_prompt, x_sample, c_prompt, c_sample, cache_k, cache_v, cache_logf, page_table, state_mlstm_c, state_mlstm_n, state_mlstm_m, state_mlstm_conv, state_hgrn, state_ffn_conv, norm_w, w_ada, b_ada, w_in, m_conv_w, m_conv_b, w_mq, w_mk, m_gate_b, m_skip, m_hnorm, f_gate_b, h_lb_logits, h_hnorm, w_pa, w_pb, w_pc, w_out, w_ffn_g, w_ffn_u, ffn_conv_w, ffn_conv_b, w_ffn_d):
    prm = dict(norm_w=norm_w, w_ada=w_ada, b_ada=b_ada, w_in=w_in, m_conv_w=m_conv_w, m_conv_b=m_conv_b,
               w_mq=w_mq, w_mk=w_mk, m_gate_b=m_gate_b, m_skip=m_skip, m_hnorm=m_hnorm, f_gate_b=f_gate_b,
               h_lb_logits=h_lb_logits, h_hnorm=h_hnorm, w_pa=w_pa, w_pb=w_pb, w_pc=w_pc, w_out=w_out,
               w_ffn_g=w_ffn_g, w_ffn_u=w_ffn_u, ffn_conv_w=ffn_conv_w, ffn_conv_b=ffn_conv_b, w_ffn_d=w_ffn_d)
    nb = x_prompt.shape[0]
    db = x_sample.shape[0]
    pad = (-(nb + db)) % 16
    c_all = jnp.concatenate([c_sample, c_prompt, jnp.zeros((pad, D_MODEL), F32)], axis=0)
    sc_all = jax.nn.silu(c_all)

    cache_k = jnp.transpose(cache_k, (0, 1, 3, 4, 2))
    cache_v = jnp.transpose(cache_v, (0, 1, 3, 4, 2))
    cache_logf = jnp.transpose(cache_logf, (0, 1, 3, 2))

    xp = x_prompt
    xs = x_sample[:, 0, :]
    pst = [[] for _ in range(9)]
    sst = [[] for _ in range(9)]
    for l in range(DEPTH):
        w = _prep_layer(l, prm)
        ada = _mm(sc_all, w['w_ada'], tm=sc_all.shape[0], tn=1024) + w['b_ada']
        xp, st = _prompt_layer(xp, ada[db:db + nb], w)
        for j in range(9):
            pst[j].append(st[j])
        xs, st = _sample_layer(l, xs, ada[:db], w, page_table, cache_k, cache_v, cache_logf,
                               state_mlstm_c[l], state_mlstm_n[l], state_mlstm_m[l], state_mlstm_conv[l],
                               state_hgrn[l], state_ffn_conv[l])
        for j in range(9):
            sst[j].append(st[j])
    p_out = [jnp.stack(s, axis=0) for s in pst]
    s_out = [jnp.stack(s, axis=0) for s in sst]
    return (xp, xs[:, None, :], *p_out, *s_out)
```

```python
import functools
import math

import numpy as np
import jax
import jax.numpy as jnp
from jax import lax
from jax.experimental import pallas as pl
from jax.experimental.pallas import tpu as pltpu

F32 = jnp.float32
BF16 = jnp.bfloat16
HIGHEST = lax.Precision.HIGHEST

D_MODEL = 1024
DEPTH = 4
PAGE_SIZE = 128
BRANCH_WIDTH = D_MODEL // 2
M_HEADS = 4
M_DK = BRANCH_WIDTH // M_HEADS
M_CONV = 4
F_HEADS = 8
F_HD = BRANCH_WIDTH // F_HEADS
H_HEADS = 4
H_DK = BRANCH_WIDTH // H_HEADS
D_FF = ((8 * D_MODEL // 3 + 127) // 128) * 128
FFN_CONV = 3
EPS = 1e-6
NEG = -1e30

LANES = 128
VMEM_LIMIT = 56 * 1024 * 1024

_SIZES = (512, 512, 512, 4, 4, 512, 512, 512, 8, 512, 512, 512, 512, 1024, 1024, 1024)
_OFF = tuple(int(v) for v in np.concatenate([[0], np.cumsum(_SIZES)]))
BLK_MX, BLK_MV, BLK_MO, BLK_FQ, BLK_FK, BLK_FV, BLK_HQ, BLK_HF, BLK_HI, BLK_HG = range(10)
COL_GA, COL_GB, COL_GC = 5120, 6144, 7168
N_MAIN = 8192
LANE_MI, LANE_MF, LANE_FF = 0, 4, 8


def _nt(a, b):
    return lax.dot_general(a, b, (((1,), (1,)), ((), ())), preferred_element_type=F32)


def _dot(a, b):
    return jnp.dot(a, b, preferred_element_type=F32)


def _log_sigmoid(x):
    return jnp.minimum(x, 0.0) - jnp.log1p(jnp.exp(-jnp.abs(x)))


def _sigmoid(x):
    return 1.0 / (1.0 + jnp.exp(-x))


def _params(*sem):
    return pltpu.CompilerParams(dimension_semantics=sem, vmem_limit_bytes=VMEM_LIMIT)


def _mm_kernel(x_ref, w_ref, o_ref, xs):
    @pl.when(pl.program_id(1) == 0)
    def _():
        xs[...] = x_ref[...].astype(BF16)

    o_ref[...] = _dot(xs[...], w_ref[...])


def _mm(x, w, *, tm, tn):
    m, k = x.shape
    n = w.shape[1]
    return pl.pallas_call(
        _mm_kernel,
        grid=(m // tm, n // tn),
        in_specs=[pl.BlockSpec((tm, k), lambda i, j: (i, 0)),
                  pl.BlockSpec((k, tn), lambda i, j: (0, j))],
        out_specs=pl.BlockSpec((tm, tn), lambda i, j: (i, j)),
        out_shape=jax.ShapeDtypeStruct((m, n), F32),
        scratch_shapes=[pltpu.VMEM((tm, k), BF16)],
        compiler_params=_params("parallel", "arbitrary"),
        name="mm",
    )(x, w)


def _pmm_kernel(x_ref, mul_ref, add_ref, w_ref, o_ref, hs):
    @pl.when(pl.program_id(1) == 0)
    def _():
        x = x_ref[...]
        y = x * lax.rsqrt(jnp.mean(x * x, axis=-1, keepdims=True) + EPS)
        hs[...] = (y * mul_ref[0] + add_ref[0]).astype(BF16)

    o_ref[...] = _dot(hs[...], w_ref[...])


def _pmm(x, mul, add, w, *, tm, tn, tiles_per_group):
    m, k = x.shape
    n = w.shape[1]
    r = mul.shape[1]
    return pl.pallas_call(
        _pmm_kernel,
        grid=(m // tm, n // tn),
        in_specs=[pl.BlockSpec((tm, k), lambda i, j: (i, 0)),
                  pl.BlockSpec((1, r, k), lambda i, j: (i // tiles_per_group, 0, 0)),
                  pl.BlockSpec((1, r, k), lambda i, j: (i // tiles_per_group, 0, 0)),
                  pl.BlockSpec((k, tn), lambda i, j: (0, j))],
        out_specs=pl.BlockSpec((tm, tn), lambda i, j: (i, j)),
        out_shape=jax.ShapeDtypeStruct((m, n), F32),
        scratch_shapes=[pltpu.VMEM((tm, k), BF16)],
        compiler_params=_params("parallel", "arbitrary"),
        name="pmm",
    )(x, mul, add, w)


def _headmm_kernel(x_ref, w_ref, o_ref):
    o_ref[...] = _dot(x_ref[...].astype(BF16), w_ref[0])


def _headmm(x, w):
    m = x.shape[0]
    h, d, _ = w.shape
    return pl.pallas_call(
        _headmm_kernel,
        grid=(h,),
        in_specs=[pl.BlockSpec((m, d), lambda i: (0, i)),
                  pl.BlockSpec((1, d, d), lambda i: (i, 0, 0))],
        out_specs=pl.BlockSpec((m, d), lambda i: (0, i)),
        out_shape=jax.ShapeDtypeStruct((m, h * d), F32),
        compiler_params=_params("parallel"),
        name="headmm",
    )(x, w)


def _merge_kernel(hm_ref, att_ref, oh_ref, ga_ref, gb_ref, gc_ref, x_ref, wpa_ref, wpb_ref, wpc_ref, wout_ref,
                  gate_ref, nw_ref, o_ref):
    ya = _dot(hm_ref[0].astype(BF16), wpa_ref[...])
    yb = _dot(att_ref[0].astype(BF16), wpb_ref[...])
    yc = _dot(oh_ref[0].astype(BF16), wpc_ref[...])
    merged = _sigmoid(ga_ref[0]) * ya + _sigmoid(gb_ref[0]) * yb + _sigmoid(gc_ref[0]) * yc
    mix = _dot(merged.astype(BF16), wout_ref[...])
    y = mix * lax.rsqrt(jnp.mean(mix * mix, axis=-1, keepdims=True) + EPS) * nw_ref[...]
    o_ref[0] = x_ref[0] + gate_ref[0] * y


def _merge_out(hm, att, oh, z, x, wpa, wpb, wpc, wout, gate, nw, *, tm):
    b, l, d = x.shape
    w = BRANCH_WIDTH
    br = pl.BlockSpec((1, tm, w), lambda i, j: (i, j, 0))
    gate_col = lambda c: pl.BlockSpec((1, tm, d), lambda i, j: (i, j, c // d))
    res = lambda shape: pl.BlockSpec(shape, lambda i, j: (0,) * len(shape))
    return pl.pallas_call(
        _merge_kernel,
        grid=(b, l // tm),
        in_specs=[br, br, br, gate_col(COL_GA), gate_col(COL_GB), gate_col(COL_GC),
                  pl.BlockSpec((1, tm, d), lambda i, j: (i, j, 0)),
                  res((w, d)), res((w, d)), res((w, d)), res((d, d)),
                  pl.BlockSpec((1, 1, d), lambda i, j: (i, 0, 0)), res((1, d))],
        out_specs=pl.BlockSpec((1, tm, d), lambda i, j: (i, j, 0)),
        out_shape=jax.ShapeDtypeStruct((b, l, d), F32),
        compiler_params=_params("parallel", "parallel"),
        name="merge_out",
    )(hm, att, oh, z, z, z, x, wpa, wpb, wpc, wout, gate, nw)


def _gelu_tanh(x):
    return 0.5 * x * (1.0 + jnp.tanh(math.sqrt(2.0 / math.pi) * (x + 0.044715 * (x * x * x))))


def _ffn_down_kernel(g_ref, u_ref, x_ref, cw_ref, cb_ref, wd_ref, gate_ref, nw_ref, o_ref, gbuf, *, tm):
    @pl.when(pl.program_id(1) == 0)
    def _():
        gbuf[0:8, :] = jnp.zeros((8, D_FF), F32)

    g = g_ref[0]
    gbuf[8:8 + tm, :] = g
    cw = cw_ref[...]
    gc = cb_ref[...] + cw[2:3, :] * g + cw[1:2, :] * gbuf[7:7 + tm, :] + cw[0:1, :] * gbuf[6:6 + tm, :]
    gbuf[0:8, :] = g[tm - 8:tm, :]
    a = (_gelu_tanh(gc) * u_ref[0]).astype(BF16)
    ffo = _dot(a, wd_ref[...])
    y = ffo * lax.rsqrt(jnp.mean(ffo * ffo, axis=-1, keepdims=True) + EPS) * nw_ref[...]
    o_ref[0] = x_ref[0] + gate_ref[0] * y


def _ffn_down(gu, x, cw, cb, wd, gate, nw, *, tm):
    b, l, d = x.shape
    return pl.pallas_call(
        functools.partial(_ffn_down_kernel, tm=tm),
        grid=(b, l // tm),
        in_specs=[pl.BlockSpec((1, tm, D_FF), lambda i, j: (i, j, 0)),
                  pl.BlockSpec((1, tm, D_FF), lambda i, j: (i, j, 1)),
                  pl.BlockSpec((1, tm, d), lambda i, j: (i, j, 0)),
                  pl.BlockSpec((FFN_CONV, D_FF), lambda i, j: (0, 0)),
                  pl.BlockSpec((1, D_FF), lambda i, j: (0, 0)),
                  pl.BlockSpec((D_FF, d), lambda i, j: (0, 0)),
                  pl.BlockSpec((1, 1, d), lambda i, j: (i, 0, 0)),
                  pl.BlockSpec((1, d), lambda i, j: (0, 0))],
        out_specs=pl.BlockSpec((1, tm, d), lambda i, j: (i, j, 0)),
        out_shape=jax.ShapeDtypeStruct((b, l, d), F32),
        scratch_shapes=[pltpu.VMEM((tm + 8, D_FF), F32)],
        compiler_params=_params("parallel", "arbitrary"),
        name="ffn_down",
    )(gu, gu, x, cw, cb, wd, gate, nw)


def _gates_kernel(zs_ref, bias_ref, mcum_ref, g_ref, cum_ref, carry):
    @pl.when(pl.program_id(1) == 0)
    def _():
        carry[...] = jnp.zeros_like(carry)

    z = zs_ref[0] + bias_ref[...]
    ls = _log_sigmoid(z)
    lane = lax.broadcasted_iota(jnp.int32, z.shape, 1)
    g_ref[0] = jnp.where(lane < LANE_MF, z, ls)
    cum = jnp.dot(mcum_ref[...], ls, precision=HIGHEST, preferred_element_type=F32) + carry[0:1, :]
    cum_ref[0] = cum
    tc = z.shape[0]
    carry[...] = jnp.broadcast_to(cum[tc - 1:tc, :], carry.shape)


def _gates(zs, bias, *, tc):
    b, l, _ = zs.shape
    mcum = jnp.asarray(np.tril(np.ones((tc, tc), np.float32)))
    return pl.pallas_call(
        _gates_kernel,
        grid=(b, l // tc),
        in_specs=[pl.BlockSpec((1, tc, LANES), lambda i, c: (i, c, 0)),
                  pl.BlockSpec((1, LANES), lambda i, c: (0, 0)),
                  pl.BlockSpec((tc, tc), lambda i, c: (0, 0))],
        out_specs=[pl.BlockSpec((1, tc, LANES), lambda i, c: (i, c, 0)),
                   pl.BlockSpec((1, tc, LANES), lambda i, c: (i, c, 0))],
        out_shape=[jax.ShapeDtypeStruct((b, l, LANES), F32),
                   jax.ShapeDtypeStruct((b, l, LANES), F32)],
        scratch_shapes=[pltpu.VMEM((8, LANES), F32)],
        compiler_params=_params("parallel", "arbitrary"),
        name="gates",
    )(zs, bias, mcum)


def _mlstm_kernel(mx_ref, mv_ref, mo_ref, gc_ref, gr_ref, cw_ref, cb_ref, wq_ref, wk_ref,
                  hn_ref, sk_ref, mcum_ref,
                  out_ref, cfin_ref, nfin_ref, mfin_ref,
                  xbuf, c_sc, n_sc, m_sc, *, c):
    ci = pl.program_id(1)

    @pl.when(ci == 0)
    def _():
        xbuf[0:8, :] = jnp.zeros((8, BRANCH_WIDTH), F32)
        c_sc[...] = jnp.zeros_like(c_sc)
        n_sc[...] = jnp.zeros_like(n_sc)
        m_sc[...] = jnp.zeros_like(m_sc)

    x = mx_ref[0]
    xbuf[8:8 + c, :] = x
    cw = cw_ref[...]
    mcv = (cb_ref[...] + cw[3:4, :] * x + cw[2:3, :] * xbuf[7:7 + c, :]
           + cw[1:2, :] * xbuf[6:6 + c, :] + cw[0:1, :] * xbuf[5:5 + c, :])
    xbuf[0:8, :] = x[c - 8:c, :]
    ca = mcv * _sigmoid(mcv)

    gcol = gc_ref[0]
    grow = gr_ref[0]
    mcum = mcum_ref[...]
    fcol = jnp.dot(mcum, gcol, precision=HIGHEST, preferred_element_type=F32)
    frow = lax.dot_general(grow, mcum, (((1,), (1,)), ((), ())), precision=HIGHEST,
                           preferred_element_type=F32)
    tri = (lax.broadcasted_iota(jnp.int32, (c, c), 0) >= lax.broadcasted_iota(jnp.int32, (c, c), 1))
    scale = M_DK ** -0.5
    mv = mv_ref[0]
    mo = mo_ref[0]

    for h in range(M_HEADS):
        sl = slice(h * M_DK, (h + 1) * M_DK)
        cah = ca[:, sl]
        cb16 = cah.astype(BF16)
        q = _dot(cb16, wq_ref[h]) * scale
        k = _dot(cb16, wk_ref[h])
        v = mv[:, sl]
        q16 = q.astype(BF16)
        v16 = v.astype(BF16)
        f_c = fcol[:, LANE_MF + h:LANE_MF + h + 1]
        ig_c = gcol[:, h:h + 1]
        f_r = frow[LANE_MF + h:LANE_MF + h + 1, :]
        ig_r = grow[h:h + 1, :]
        m_prev = m_sc[h:h + 1, 0:1]

        d = jnp.where(tri, f_c - f_r + ig_r, NEG)
        inter = f_c + m_prev
        mt = jnp.maximum(jnp.max(d, axis=-1, keepdims=True), inter)
        a = _nt(q16, k.astype(BF16)) * jnp.exp(d - mt)
        ei = jnp.exp(inter - mt)
        num = _dot(a.astype(BF16), v16) + ei * _dot(q16, c_sc[h].astype(BF16))
        den = jnp.sum(a, axis=-1, keepdims=True) + ei * jnp.sum(q * n_sc[h:h + 1, :], axis=-1, keepdims=True)
        hh = num / jnp.maximum(jnp.abs(den), jnp.exp(-mt))

        fl = f_c[c - 1:c, :]
        wl_c = fl - f_c + ig_c
        wl_r = fl - f_r + ig_r
        mn = jnp.maximum(fl + m_prev, jnp.max(wl_r, axis=-1, keepdims=True))
        dec = jnp.exp(fl + m_prev - mn)
        kw = k * jnp.exp(wl_c - mn)
        c_sc[h] = dec * c_sc[h] + _dot(kw.T.astype(BF16), v16)
        n_sc[h:h + 1, :] = dec * n_sc[h:h + 1, :] + jnp.sum(kw, axis=0, keepdims=True)
        m_sc[h:h + 1, :] = jnp.broadcast_to(mn, (1, LANES))

        hn = hh * lax.rsqrt(jnp.mean(hh * hh, axis=-1, keepdims=True) + EPS)
        hn = hn * hn_ref[:, sl] + sk_ref[:, sl] * cah
        moh = mo[:, sl]
        out_ref[0, :, sl] = hn * (moh * _sigmoid(moh))

    @pl.when(ci == pl.num_programs(1) - 1)
    def _():
        cfin_ref[0] = c_sc[...]
        nfin_ref[0] = n_sc[...]
        mfin_ref[0] = m_sc[...]


def _mlstm_prompt(z, gcol, grow, cw, cb, wq, wk, hnorm, skip, *, c):
    b, l, _ = z.shape
    mcum = jnp.asarray(np.tril(np.ones((c, c), np.float32)))
    w = BRANCH_WIDTH
    full = lambda shape: pl.BlockSpec(shape, lambda i, j: (0,) * len(shape))
    return pl.pallas_call(
        functools.partial(_mlstm_kernel, c=c),
        grid=(b, l // c),
        in_specs=[pl.BlockSpec((1, c, w), lambda i, j: (i, j, BLK_MX)),
                  pl.BlockSpec((1, c, w), lambda i, j: (i, j, BLK_MV)),
                  pl.BlockSpec((1, c, w), lambda i, j: (i, j, BLK_MO)),
                  pl.BlockSpec((1, c, LANES), lambda i, j: (i, j, 0)),
                  pl.BlockSpec((1, 8, c), lambda i, j: (i, 0, j)),
                  full((M_CONV, w)), full((1, w)),
                  full((M_HEADS, M_DK, M_DK)), full((M_HEADS, M_DK, M_DK)),
                  full((1, w)), full((1, w)), full((c, c))],
        out_specs=[pl.BlockSpec((1, c, w), lambda i, j: (i, j, 0)),
                   pl.BlockSpec((1, M_HEADS, M_DK, M_DK), lambda i, j: (i, 0, 0, 0)),
                   pl.BlockSpec((1, 8, LANES), lambda i, j: (i, 0, 0)),
                   pl.BlockSpec((1, 8, LANES), lambda i, j: (i, 0, 0))],
        out_shape=[jax.ShapeDtypeStruct((b, l, w), F32),
                   jax.ShapeDtypeStruct((b, M_HEADS, M_DK, M_DK), F32),
                   jax.ShapeDtypeStruct((b, 8, LANES), F32),
                   jax.ShapeDtypeStruct((b, 8, LANES), F32)],
        scratch_shapes=[pltpu.VMEM((c + 8, w), F32),
                        pltpu.VMEM((M_HEADS, M_DK, M_DK), F32),
                        pltpu.VMEM((8, LANES), F32),
                        pltpu.VMEM((8, LANES), F32)],
        compiler_params=_params("parallel", "arbitrary"),
        name="mlstm_prompt",
    )(z, z, z, gcol, grow, cw, cb, wq, wk, hnorm, skip, mcum)


def _fox_step(masked, q_ref, k_ref, v_ref, fq_ref, fk_ref, m_sc, l_sc, acc_sc, tq, tk):
    q = q_ref[0] * (F_HD ** -0.5)
    k = k_ref[0].astype(BF16)
    v = v_ref[0].astype(BF16)
    fq = fq_ref[0]
    fk = fk_ref[0]
    lo = lax.broadcasted_iota(jnp.int32, (tq, LANES), 1) < F_HD
    if masked:
        tri = (lax.broadcasted_iota(jnp.int32, (tq, tk), 0) >= lax.broadcasted_iota(jnp.int32, (tq, tk), 1))
    for pr in range(F_HEADS // 2):
        sl = slice(pr * LANES, (pr + 1) * LANES)
        q2, k2, v2 = q[:, sl], k[:, sl], v[:, sl]
        alphas, pvs = [], []
        for hh in range(2):
            h = 2 * pr + hh
            qm = jnp.where(lo if hh == 0 else jnp.logical_not(lo), q2, 0.0).astype(BF16)
            s = _nt(qm, k2) + fq[:, LANE_FF + h:LANE_FF + h + 1] - fk[h:h + 1, :]
            if masked:
                s = jnp.where(tri, s, NEG)
            m_prev = m_sc[:, h:h + 1]
            m_new = jnp.maximum(m_prev, jnp.max(s, axis=-1, keepdims=True))
            alpha = jnp.exp(m_prev - m_new)
            p = jnp.exp(s - m_new)
            l_sc[:, h:h + 1] = alpha * l_sc[:, h:h + 1] + jnp.sum(p, axis=-1, keepdims=True)
            m_sc[:, h:h + 1] = m_new
            pvs.append(_dot(p.astype(BF16), v2))
            alphas.append(alpha)
        acc_sc[pr] = acc_sc[pr] * jnp.where(lo, alphas[0], alphas[1]) + jnp.where(lo, pvs[0], pvs[1])


def _fox_kernel(qi_ref, kj_ref, q_ref, k_ref, v_ref, fq_ref, fk_ref, o_ref, m_sc, l_sc, acc_sc, *, tq, tk):
    s = pl.program_id(1)
    i = qi_ref[s]
    j = kj_ref[s]

    @pl.when(j == 0)
    def _():
        m_sc[...] = jnp.full(m_sc.shape, NEG, F32)
        l_sc[...] = jnp.zeros_like(l_sc)
        acc_sc[...] = jnp.zeros_like(acc_sc)

    @pl.when(j < i)
    def _():
        _fox_step(False, q_ref, k_ref, v_ref, fq_ref, fk_ref, m_sc, l_sc, acc_sc, tq, tk)

    @pl.when(j == i)
    def _():
        _fox_step(True, q_ref, k_ref, v_ref, fq_ref, fk_ref, m_sc, l_sc, acc_sc, tq, tk)
        lo = lax.broadcasted_iota(jnp.int32, (tq, LANES), 1) < F_HD
        for pr in range(F_HEADS // 2):
            inv = jnp.where(lo, 1.0 / l_sc[:, 2 * pr:2 * pr + 1], 1.0 / l_sc[:, 2 * pr + 1:2 * pr + 2])
            o_ref[0, :, pr * LANES:(pr + 1) * LANES] = acc_sc[pr] * inv


def _fox_prompt(z, cum, fkt, *, t):
    b, l, _ = z.shape
    nb = l // t
    qi = np.array([i for i in range(nb) for _ in range(i + 1)], np.int32)
    kj = np.array([j for i in range(nb) for j in range(i + 1)], np.int32)
    w = BRANCH_WIDTH
    grid_spec = pltpu.PrefetchScalarGridSpec(
        num_scalar_prefetch=2,
        grid=(b, len(qi)),
        in_specs=[pl.BlockSpec((1, t, w), lambda i, s, qi, kj: (i, qi[s], BLK_FQ)),
                  pl.BlockSpec((1, t, w), lambda i, s, qi, kj: (i, kj[s], BLK_FK)),
                  pl.BlockSpec((1, t, w), lambda i, s, qi, kj: (i, kj[s], BLK_FV)),
                  pl.BlockSpec((1, t, LANES), lambda i, s, qi, kj: (i, qi[s], 0)),
                  pl.BlockSpec((1, 8, t), lambda i, s, qi, kj: (i, 0, kj[s]))],
        out_specs=pl.BlockSpec((1, t, w), lambda i, s, qi, kj: (i, qi[s], 0)),
        scratch_shapes=[pltpu.VMEM((t, LANES), F32), pltpu.VMEM((t, LANES), F32),
                        pltpu.VMEM((F_HEADS // 2, t, LANES), F32)])
    return pl.pallas_call(
        functools.partial(_fox_kernel, tq=t, tk=t),
        grid_spec=grid_spec,
        out_shape=jax.ShapeDtypeStruct((b, l, w), F32),
        compiler_params=_params("parallel", "arbitrary"),
        name="fox_prompt",
    )(jnp.asarray(qi), jnp.asarray(kj), z, z, z, cum, fkt)


HG_C = 128
HG_LEVELS = 7


def _hgrn_mats(c):
    mats = [np.tril(np.ones((c, c), np.float32))]
    for lev in range(HG_LEVELS):
        s = (c // 2) >> lev
        m = np.zeros((c, c), np.float32)
        for t in range(c):
            ref = (t // (2 * s)) * 2 * s + s - 1
            if t % (2 * s) >= s:
                m[t, ref + 1:t + 1] = 1.0
            else:
                m[t, t + 1:ref + 1] = 1.0
        mats.append(m)
    mats.append(np.triu(np.ones((c, c), np.float32), 1))
    return np.concatenate(mats, axis=0)


def _hgrn_kernel(hq_ref, hf_ref, hi_ref, hg_ref, lb_ref, hn_ref, mall_ref, o_ref, sfin_ref, st_sc):
    c = HG_C
    ci = pl.program_id(2)

    @pl.when(ci == 0)
    def _():
        st_sc[...] = jnp.zeros_like(st_sc)

    hf = hf_ref[0]
    lb = lb_ref[...]
    la = jnp.log(lb + (1.0 - lb) * _sigmoid(hf))
    k = (1.0 - lb) * _sigmoid(-hf)
    hq = hq_ref[0]
    q = hq * _sigmoid(hq)
    v16 = hi_ref[0].astype(BF16)

    la_hi = la.astype(BF16)
    la_r = la - la_hi.astype(F32)
    la_mid = la_r.astype(BF16)
    la_lo = (la_r - la_mid.astype(F32)).astype(BF16)
    mall = mall_ref[...]
    args = _dot(mall, la_hi) + _dot(mall, la_mid) + _dot(mall, la_lo)
    b = args[0:c]
    ti = lax.broadcasted_iota(jnp.int32, (c, c), 0)
    si = lax.broadcasted_iota(jnp.int32, (c, c), 1)
    rowi = lax.broadcasted_iota(jnp.int32, (c, H_DK), 0)
    a = jnp.where(ti == si, _nt(q.astype(BF16), k.astype(BF16)), 0.0)
    for lev in range(HG_LEVELS):
        sh = int(math.log2(c // 2)) - lev
        upper = ((rowi >> sh) & 1) == 1
        x16 = (jnp.where(upper, q, k) * jnp.exp(args[(lev + 1) * c:(lev + 2) * c])).astype(BF16)
        p = _nt(x16, x16)
        mask = ((ti >> (sh + 1)) == (si >> (sh + 1))) & (((ti >> sh) & 1) == 1) & (((si >> sh) & 1) == 0)
        a = a + jnp.where(mask, p, 0.0)

    st = st_sc[...]
    o = _dot(a.astype(BF16), v16) + _nt((q * jnp.exp(b)).astype(BF16), st.astype(BF16))
    kt = k * jnp.exp(args[(HG_LEVELS + 1) * c:(HG_LEVELS + 2) * c])
    st_new = st * jnp.exp(b[c - 1:c, :]) + _dot(hi_ref[0].T.astype(BF16), kt.astype(BF16))
    st_sc[...] = st_new

    hg = hg_ref[0]
    on = o * lax.rsqrt(jnp.mean(o * o, axis=-1, keepdims=True) + EPS)
    o_ref[0] = on * hn_ref[...] * (hg * _sigmoid(hg))

    @pl.when(ci == pl.num_programs(2) - 1)
    def _():
        sfin_ref[0, 0] = st_new.T


def _hgrn_prompt(z, lb, hnorm):
    b, l, _ = z.shape
    c = HG_C
    mall = jnp.asarray(_hgrn_mats(c)).astype(BF16)
    blk = lambda base: pl.BlockSpec((1, c, H_DK), lambda i, h, j: (i, j, base * 4 + h))
    return pl.pallas_call(
        _hgrn_kernel,
        grid=(b, H_HEADS, l // c),
        in_specs=[blk(BLK_HQ), blk(BLK_HF), blk(BLK_HI), blk(BLK_HG),
                  pl.BlockSpec((1, H_DK), lambda i, h, j: (0, h)),
                  pl.BlockSpec((1, H_DK), lambda i, h, j: (0, h)),
                  pl.BlockSpec(mall.shape, lambda i, h, j: (0, 0))],
        out_specs=[pl.BlockSpec((1, c, H_DK), lambda i, h, j: (i, j, h)),
                   pl.BlockSpec((1, 1, H_DK, H_DK), lambda i, h, j: (i, h, 0, 0))],
        out_shape=[jax.ShapeDtypeStruct((b, l, BRANCH_WIDTH), F32),
                   jax.ShapeDtypeStruct((b, H_HEADS, H_DK, H_DK), F32)],
        scratch_shapes=[pltpu.VMEM((H_DK, H_DK), F32)],
        compiler_params=_params("parallel", "parallel", "arbitrary"),
        name="hgrn_prompt",
    )(z, z, z, z, lb, hnorm, mall)


def _state_step_kernel(st_ref, at_ref, bt_ref, qt_ref, v_ref, so_ref, o_ref, *, gb):
    at = at_ref[0]
    bt = bt_ref[0]
    qt = qt_ref[0]
    v = v_ref[...]
    for j in range(gb):
        s_new = st_ref[j] * at[:, j:j + 1] + bt[:, j:j + 1] * v[j:j + 1, :]
        so_ref[j] = s_new
        o_ref[j:j + 1, :] = jnp.sum(s_new * qt[:, j:j + 1], axis=0, keepdims=True)


def _state_step(state, a, b, q, v, *, gb=32):
    g, dk, dv = state.shape
    tr = lambda x: x.reshape(g // gb, gb, dk).transpose(0, 2, 1)
    return pl.pallas_call(
        functools.partial(_state_step_kernel, gb=gb),
        grid=(g // gb,),
        in_specs=[pl.BlockSpec((gb, dk, dv), lambda i: (i, 0, 0)),
                  pl.BlockSpec((1, dk, gb), lambda i: (i, 0, 0)),
                  pl.BlockSpec((1, dk, gb), lambda i: (i, 0, 0)),
                  pl.BlockSpec((1, dk, gb), lambda i: (i, 0, 0)),
                  pl.BlockSpec((gb, dv), lambda i: (i, 0))],
        out_specs=[pl.BlockSpec((gb, dk, dv), lambda i: (i, 0, 0)),
                   pl.BlockSpec((gb, dv), lambda i: (i, 0))],
        out_shape=[jax.ShapeDtypeStruct((g, dk, dv), F32),
                   jax.ShapeDtypeStruct((g, dv), F32)],
        compiler_params=_params("parallel"),
        name="state_step",
    )(state, tr(a), tr(b), tr(q), v)


def _fox_sample_kernel(pt_ref, q_ref, kn_ref, lfn_ref, msuf_ref, *rest, n_pages):
    k_refs = rest[:n_pages]
    v_refs = rest[n_pages:2 * n_pages]
    lf_refs = rest[2 * n_pages:3 * n_pages]
    ot_ref, st_ref = rest[3 * n_pages:]

    q = q_ref[0] * (F_HD ** -0.5)
    eye = (lax.broadcasted_iota(jnp.int32, (F_HD, F_HD), 0) == lax.broadcasted_iota(jnp.int32, (F_HD, F_HD), 1))
    ones = jnp.ones((F_HD, PAGE_SIZE), F32)
    qb = [jnp.dot(jnp.where(eye, q[h:h + 1, :], 0.0), ones, precision=HIGHEST, preferred_element_type=F32)
          for h in range(F_HEADS)]
    hrow = lax.broadcasted_iota(jnp.int32, (F_HEADS, PAGE_SIZE), 0)
    lfn = lfn_ref[0]

    carry = jnp.zeros((F_HEADS, 1), F32)
    tiles = [None] * n_pages
    for pg in reversed(range(n_pages)):
        lf = lf_refs[pg][0, 0]
        after = jnp.dot(lf, msuf_ref[...], precision=HIGHEST, preferred_element_type=F32) + carry
        carry = carry + jnp.sum(lf, axis=-1, keepdims=True)
        s = after + lfn
        for h in range(F_HEADS):
            row = jnp.sum(k_refs[pg][0, 0, h] * qb[h], axis=0, keepdims=True)
            s = s + jnp.where(hrow == h, row, 0.0)
        tiles[pg] = s

    s_new = jnp.sum(q * kn_ref[0], axis=-1, keepdims=True)
    m = s_new
    for t in tiles:
        m = jnp.maximum(m, jnp.max(t, axis=-1, keepdims=True))
    p_new = jnp.exp(s_new - m)
    l = p_new
    probs = []
    for t in tiles:
        p = jnp.exp(t - m)
        l = l + jnp.sum(p, axis=-1, keepdims=True)
        probs.append(p)

    lane = lax.broadcasted_iota(jnp.int32, (F_HD, LANES), 1)
    ot = jnp.zeros((F_HD, LANES), F32)
    for h in range(F_HEADS):
        acc = jnp.zeros((F_HD, PAGE_SIZE), F32)
        for pg in range(n_pages):
            acc = acc + v_refs[pg][0, 0, h] * probs[pg][h:h + 1, :]
        ot = jnp.where(lane == h, jnp.sum(acc, axis=-1, keepdims=True), ot)
    ot_ref[0] = ot
    slane = lax.broadcasted_iota(jnp.int32, (F_HEADS, LANES), 1)
    st_ref[0] = jnp.where(slane == 0, l, jnp.where(slane == 1, p_new, 0.0))


def _fox_sample(layer, page_table, cache_kt, cache_vt, cache_lft, q, kn, vn, lff):
    db, n_pages = page_table.shape
    msuf = jnp.asarray(np.tril(np.ones((PAGE_SIZE, PAGE_SIZE), np.float32), -1))
    lfn = jnp.broadcast_to(lff[:, :, None], (db, F_HEADS, LANES))

    def kv_spec(pg):
        return pl.BlockSpec((1, 1, F_HEADS, F_HD, PAGE_SIZE), lambda i, pt: (layer, pt[i, pg], 0, 0, 0))

    def lf_spec(pg):
        return pl.BlockSpec((1, 1, F_HEADS, PAGE_SIZE), lambda i, pt: (layer, pt[i, pg], 0, 0))

    tok = pl.BlockSpec((1, F_HEADS, F_HD), lambda i, pt: (i, 0, 0))
    grid_spec = pltpu.PrefetchScalarGridSpec(
        num_scalar_prefetch=1,
        grid=(db,),
        in_specs=[tok, tok,
                  pl.BlockSpec((1, F_HEADS, LANES), lambda i, pt: (i, 0, 0)),
                  pl.BlockSpec((PAGE_SIZE, PAGE_SIZE), lambda i, pt: (0, 0))]
        + [kv_spec(pg) for pg in range(n_pages)] + [kv_spec(pg) for pg in range(n_pages)]
        + [lf_spec(pg) for pg in range(n_pages)],
        out_specs=[pl.BlockSpec((1, F_HD, LANES), lambda i, pt: (i, 0, 0)),
                   pl.BlockSpec((1, F_HEADS, LANES), lambda i, pt: (i, 0, 0))])
    ot, stats = pl.pallas_call(
        functools.partial(_fox_sample_kernel, n_pages=n_pages),
        grid_spec=grid_spec,
        out_shape=[jax.ShapeDtypeStruct((db, F_HD, LANES), F32),
                   jax.ShapeDtypeStruct((db, F_HEADS, LANES), F32)],
        compiler_params=_params("parallel"),
        name="fox_sample",
    )(page_table, q, kn, lfn, msuf, *([cache_kt] * n_pages), *([cache_vt] * n_pages), *([cache_lft] * n_pages))
    past = jnp.swapaxes(ot[:, :, :F_HEADS], 1, 2)
    return (past + stats[:, :, 1:2] * vn) / stats[:, :, 0:1]


def _rms(x, g):
    return x * lax.rsqrt(jnp.mean(x * x, axis=-1, keepdims=True) + EPS) * g


def _head_rms(o, g, heads):
    shp = o.shape
    o = o.reshape(shp[:-1] + (heads, shp[-1] // heads))
    y = o * lax.rsqrt(jnp.mean(o * o, axis=-1, keepdims=True) + EPS)
    return y.reshape(shp) * g


def _prep_layer(l, prm):
    w_in = prm['w_in'][l]
    seg = lambda i: w_in[:, _OFF[i]:_OFF[i + 1]]
    main = jnp.concatenate([seg(0), seg(1), seg(2), seg(5), seg(6), seg(7), seg(9), seg(10), seg(11), seg(12),
                            seg(13), seg(14), seg(15)], axis=1).astype(BF16)
    small = jnp.concatenate([seg(3), seg(4), seg(8), jnp.zeros((D_MODEL, LANES - 16), F32)], axis=1).astype(BF16)
    gate_bias = jnp.concatenate([prm['m_gate_b'][l], prm['f_gate_b'][l], jnp.zeros((LANES - 16,), F32)])[None, :]
    pl_sm = jax.nn.softmax(prm['h_lb_logits'].astype(F32), axis=0)
    cum = jnp.cumsum(pl_sm, axis=0)
    lb = (cum[l] - cum[0])[None, :]
    return dict(
        w_main=main, w_small=small, gate_bias=gate_bias, lb=lb,
        w_ada=prm['w_ada'][l].astype(BF16), b_ada=prm['b_ada'][l], nw=prm['norm_w'][l],
        cw=prm['m_conv_w'][l], cb=prm['m_conv_b'][l][None, :],
        wq=prm['w_mq'][l].astype(BF16), wk=prm['w_mk'][l].astype(BF16),
        hnorm_m=prm['m_hnorm'][l][None, :], skip=prm['m_skip'][l][None, :],
        hnorm_h=prm['h_hnorm'][l][None, :],
        w_pa=prm['w_pa'][l].astype(BF16), w_pb=prm['w_pb'][l].astype(BF16), w_pc=prm['w_pc'][l].astype(BF16),
        w_out=prm['w_out'][l].astype(BF16),
        w_gu=jnp.concatenate([prm['w_ffn_g'][l], prm['w_ffn_u'][l]], axis=1).astype(BF16),
        fcw=prm['ffn_conv_w'][l], fcb=prm['ffn_conv_b'][l], w_d=prm['w_ffn_d'][l].astype(BF16))


def _prompt_layer(x, ada, w):
    b, l, d = x.shape
    m = b * l
    tm = 1024
    sh1, sc1, g1, sh2, sc2, g2 = jnp.split(ada[:, None, :], 6, axis=-1)
    x2 = x.reshape(m, d)
    mul1 = w['nw'][0] * (1.0 + sc1)
    z = _pmm(x2, mul1, sh1, w['w_main'], tm=tm, tn=1024, tiles_per_group=l // tm).reshape(b, l, N_MAIN)
    zs = _pmm(x2, mul1, sh1, w['w_small'], tm=tm, tn=LANES, tiles_per_group=l // tm).reshape(b, l, LANES)

    gcol, cum = _gates(zs, w['gate_bias'], tc=512)
    grow = jnp.swapaxes(gcol[:, :, 0:8], 1, 2)

    hm, c1, n1, m1 = _mlstm_prompt(z, gcol, grow, w['cw'], w['cb'], w['wq'], w['wk'], w['hnorm_m'], w['skip'], c=128)
    fkt = jnp.swapaxes(cum[:, :, LANE_FF:LANE_FF + 8], 1, 2)
    att = _fox_prompt(z, cum, fkt, t=512)
    oh, s1 = _hgrn_prompt(z, w['lb'], w['hnorm_h'])

    x = _merge_out(hm, att, oh, z, x, w['w_pa'], w['w_pb'], w['w_pc'], w['w_out'], 1.0 + g1,
                   w['nw'][1][None, :], tm=256)

    mul2 = w['nw'][2] * (1.0 + sc2)
    gu = _pmm(x.reshape(m, d), mul2, sh2, w['w_gu'], tm=tm, tn=512, tiles_per_group=l // tm).reshape(b, l, 2 * D_FF)
    x = _ffn_down(gu, x, w['fcw'], w['fcb'][None, :], w['w_d'], 1.0 + g2, w['nw'][3][None, :], tm=256)

    k = z[..., BLK_FK * 512:(BLK_FK + 1) * 512].reshape(b, l, F_HEADS, F_HD)
    v = z[..., BLK_FV * 512:(BLK_FV + 1) * 512].reshape(b, l, F_HEADS, F_HD)
    lff = gcol[..., LANE_FF:LANE_FF + F_HEADS]
    mconv = z[:, l - (M_CONV - 1):, 0:BRANCH_WIDTH]
    fconv = gu[:, l - (FFN_CONV - 1):, :D_FF]
    st = (k, v, lff, c1, n1[:, :M_HEADS, :], m1[:, :M_HEADS, 0], mconv, s1, fconv)
    return x, st


def _sample_layer(l, x, ada, w, page_table, cache_k, cache_v, cache_logf, st_c, st_n, st_m, st_mconv, st_h, st_fconv):
    db, d = x.shape
    sh1, sc1, g1, sh2, sc2, g2 = jnp.split(ada, 6, axis=-1)
    mul1 = (w['nw'][0] * (1.0 + sc1))[None]
    z = _pmm(x, mul1, sh1[None], w['w_main'], tm=db, tn=1024, tiles_per_group=1)
    zs = _pmm(x, mul1, sh1[None], w['w_small'], tm=db, tn=LANES, tiles_per_group=1) + w['gate_bias']
    blk = lambda i: z[:, i * 512:(i + 1) * 512]

    mx = blk(BLK_MX)
    cw = w['cw']
    mcv = w['cb'] + cw[0] * st_mconv[:, 0] + cw[1] * st_mconv[:, 1] + cw[2] * st_mconv[:, 2] + cw[3] * mx
    mconv_new = jnp.concatenate([st_mconv[:, 1:], mx[:, None, :]], axis=1)
    ca = jax.nn.silu(mcv)
    mq = _headmm(ca, w['wq']).reshape(db, M_HEADS, M_DK) * (M_DK ** -0.5)
    mk = _headmm(ca, w['wk']).reshape(db, M_HEADS, M_DK)
    ig = zs[:, LANE_MI:LANE_MI + M_HEADS]
    lfm = jax.nn.log_sigmoid(zs[:, LANE_MF:LANE_MF + M_HEADS])
    mn = jnp.maximum(ig, lfm + st_m)
    wk_ = jnp.exp(ig - mn)[..., None]
    dec = jnp.exp(lfm + st_m - mn)[..., None]
    g_ = db * M_HEADS
    c_new, onum = _state_step(st_c.reshape(g_, M_DK, M_DK),
                              jnp.broadcast_to(dec, (db, M_HEADS, M_DK)).reshape(g_, M_DK),
                              (wk_ * mk).reshape(g_, M_DK), mq.reshape(g_, M_DK),
                              blk(BLK_MV).reshape(g_, M_DK))
    n_new = dec * st_n + wk_ * mk
    den = jnp.sum(mq * n_new, axis=-1, keepdims=True)
    hm = onum.reshape(db, M_HEADS, M_DK) / jnp.maximum(jnp.abs(den), jnp.exp(-mn)[..., None])
    hm = (_head_rms(hm.reshape(db, -1), w['hnorm_m'], M_HEADS) + w['skip'] * ca) * jax.nn.silu(blk(BLK_MO))
    ya = _mm(hm, w['w_pa'], tm=db, tn=1024)

    fq = blk(BLK_FQ).reshape(db, F_HEADS, F_HD)
    fk = blk(BLK_FK).reshape(db, F_HEADS, F_HD)
    fv = blk(BLK_FV).reshape(db, F_HEADS, F_HD)
    lff = jax.nn.log_sigmoid(zs[:, LANE_FF:LANE_FF + F_HEADS])
    att = _fox_sample(l, page_table, cache_k, cache_v, cache_logf, fq, fk, fv, lff)
    yb = _mm(att.reshape(db, -1), w['w_pb'], tm=db, tn=1024)

    hf = blk(BLK_HF)
    lb = w['lb']
    f = lb + (1.0 - lb) * jax.nn.sigmoid(hf)
    hk = (1.0 - lb) * jax.nn.sigmoid(-hf)
    hq = jax.nn.silu(blk(BLK_HQ))
    g2_ = db * H_HEADS
    s_new, oh = _state_step(st_h.reshape(g2_, H_DK, H_DK), f.reshape(g2_, H_DK), hk.reshape(g2_, H_DK),
                            hq.reshape(g2_, H_DK), blk(BLK_HI).reshape(g2_, H_DK))
    oh = _head_rms(oh.reshape(db, -1), w['hnorm_h'], H_HEADS) * jax.nn.silu(blk(BLK_HG))
    yc = _mm(oh, w['w_pc'], tm=db, tn=1024)

    merged = (jax.nn.sigmoid(z[:, COL_GA:COL_GA + d]) * ya + jax.nn.sigmoid(z[:, COL_GB:COL_GB + d]) * yb
              + jax.nn.sigmoid(z[:, COL_GC:COL_GC + d]) * yc)
    mix = _mm(merged, w['w_out'], tm=db, tn=1024)
    x = x + (1.0 + g1) * _rms(mix, w['nw'][1])

    mul2 = (w['nw'][2] * (1.0 + sc2))[None]
    gu = _pmm(x, mul2, sh2[None], w['w_gu'], tm=db, tn=512, tiles_per_group=1)
    g, u = gu[:, :D_FF], gu[:, D_FF:]
    fcw = w['fcw']
    gc = w['fcb'] + fcw[0] * st_fconv[:, 0] + fcw[1] * st_fconv[:, 1] + fcw[2] * g
    fconv_new = jnp.concatenate([st_fconv[:, 1:], g[:, None, :]], axis=1)
    ffo = _mm(jax.nn.gelu(gc) * u, w['w_d'], tm=db, tn=1024)
    x = x + (1.0 + g2) * _rms(ffo, w['nw'][3])

    st = (fk[:, None], fv[:, None], lff[:, None], c_new.reshape(db, M_HEADS, M_DK, M_DK), n_new, mn,
          mconv_new, s_new.reshape(db, H_HEADS, H_DK, H_DK), fconv_new)
    return x, st


def kernel(x_prompt, x_sample, c_prompt, c_sample, cache_k, cache_v, cache_logf, page_table, state_mlstm_c, state_mlstm_n, state_mlstm_m, state_mlstm_conv, state_hgrn, state_ffn_conv, norm_w, w_ada, b_ada, w_in, m_conv_w, m_conv_b, w_mq, w_mk, m_gate_b, m_skip, m_hnorm, f_gate_b, h_lb_logits, h_hnorm, w_pa, w_pb, w_pc, w_out, w_ffn_g, w_ffn_u, ffn_conv_w, ffn_conv_b, w_ffn_d):
    prm = dict(norm_w=norm_w, w_ada=w_ada, b_ada=b_ada, w_in=w_in, m_conv_w=m_conv_w, m_conv_b=m_conv_b,
               w_mq=w_mq, w_mk=w_mk, m_gate_b=m_gate_b, m_skip=m_skip, m_hnorm=m_hnorm, f_gate_b=f_gate_b,
               h_lb_logits=h_lb_logits, h_hnorm=h_hnorm, w_pa=w_pa, w_pb=w_pb, w_pc=w_pc, w_out=w_out,
               w_ffn_g=w_ffn_g, w_ffn_u=w_ffn_u, ffn_conv_w=ffn_conv_w, ffn_conv_b=ffn_conv_b, w_ffn_d=w_ffn_d)
    nb = x_prompt.shape[0]
    db = x_sample.shape[0]
    pad = (-(nb + db)) % 16
    c_all = jnp.concatenate([c_sample, c_prompt, jnp.zeros((pad, D_MODEL), F32)], axis=0)
    sc_all = jax.nn.silu(c_all)

    cache_k = jnp.transpose(cache_k, (0, 1, 3, 4, 2))
    cache_v = jnp.transpose(cache_v, (0, 1, 3, 4, 2))
    cache_logf = jnp.transpose(cache_logf, (0, 1, 3, 2))

    xp = x_prompt
    xs = x_sample[:, 0, :]
    pst = [[] for _ in range(9)]
    sst = [[] for _ in range(9)]
    for l in range(DEPTH):
        w = _prep_layer(l, prm)
        ada = _mm(sc_all, w['w_ada'], tm=sc_all.shape[0], tn=1024) + w['b_ada']
        xp, st = _prompt_layer(xp, ada[db:db + nb], w)
        for j in range(9):
            pst[j].append(st[j])
        xs, st = _sample_layer(l, xs, ada[:db], w, page_table, cache_k, cache_v, cache_logf,
                               state_mlstm_c[l], state_mlstm_n[l], state_mlstm_m[l], state_mlstm_conv[l],
                               state_hgrn[l], state_ffn_conv[l])
        for j in range(9):
            sst[j].append(st[j])
    p_out = [jnp.stack(s, axis=0) for s in pst]
    s_out = [jnp.stack(s, axis=0) for s in sst]
    return (xp, xs[:, None, :], *p_out, *s_out)
```
